```python
import jax, jax.numpy as jnp
from jax import lax
import numpy as np

D_MODEL = 1024
BATCH = 4
SEQ = 4096
DEPTH = 4

POOL_WINDOWS = (2, 4, 8, 16)
POOL_GROUP = 128
POOL_WIDTH = POOL_GROUP * len(POOL_WINDOWS)
ATTN_HEADS = 8
HEAD_DIM = 64
ATTN_WIDTH = ATTN_HEADS * HEAD_DIM
MOBA_BLOCK = 256
MOBA_TOPK = 3
Q_CHUNK = 64
RNN_BLOCKS = 10
RNN_BLOCK = 128
RNN_WIDTH = RNN_BLOCKS * RNN_BLOCK
RNN_CONV = 4
RG_C = 8.0
D_FF = 2816
FFN_CONV = 3
EPS = 1e-6
NEG_INF = -1e30
N_EVEN = (DEPTH + 1) // 2
N_ODD = DEPTH // 2

kernel_name = "hybrid_pool_moba_rglru_convffn"


def rms_norm(x, g):
    x32 = x.astype(jnp.float32)
    y = x32 * lax.rsqrt(jnp.mean(x32 * x32, axis=-1, keepdims=True) + EPS)
    return (y * g.astype(jnp.float32)).astype(x.dtype)


def causal_dwconv(x, w, b):
    K, C = w.shape
    y = lax.conv_general_dilated(
        x, w[:, None, :].astype(x.dtype), window_strides=(1,), padding=[(K - 1, 0)],
        dimension_numbers=('NWC', 'WIO', 'NWC'), feature_group_count=C)
    return y + b.astype(x.dtype)


def multiscale_pool_minus_identity(u):
    S = u.shape[1]
    u32 = u.astype(jnp.float32)
    cs = jnp.cumsum(u32, axis=1)
    count = jnp.arange(1, S + 1, dtype=jnp.float32)[None, :, None]
    outs = []
    for gi, w in enumerate(POOL_WINDOWS):
        sl = slice(gi * POOL_GROUP, (gi + 1) * POOL_GROUP)
        c = cs[..., sl]
        prev = jnp.pad(c, ((0, 0), (w, 0), (0, 0)))[:, :S]
        outs.append((c - prev) / jnp.minimum(count, float(w)) - u32[..., sl])
    return jnp.concatenate(outs, axis=-1)


def head_rms(t, g):
    t32 = t.astype(jnp.float32)
    return t32 * lax.rsqrt(jnp.mean(t32 * t32, axis=-1, keepdims=True) + EPS) * g.astype(jnp.float32)


def moba_attention(q, k, v, q_gain, k_gain):
    B, S, H, Dh = q.shape
    nb = -(-S // MOBA_BLOCK)
    s_pad = nb * MOBA_BLOCK
    pad = ((0, 0), (0, s_pad - S), (0, 0), (0, 0))
    qn = jnp.pad(head_rms(q, q_gain), pad).transpose(0, 2, 1, 3)
    kn = jnp.pad(head_rms(k, k_gain), pad).transpose(0, 2, 1, 3)
    vp = jnp.pad(v.astype(jnp.float32), pad).transpose(0, 2, 1, 3)
    k_blocks = kn.reshape(B, H, nb, MOBA_BLOCK, Dh)
    v_blocks = vp.reshape(B, H, nb, MOBA_BLOCK, Dh)
    slopes = 2.0 ** (-8.0 * jnp.arange(1, H + 1, dtype=jnp.float32) / H)
    scale = Dh ** -0.5
    topk = min(MOBA_TOPK, nb - 1)
    q_blk = jnp.arange(s_pad) // MOBA_BLOCK
    if topk > 0:
        k_mean = jnp.mean(k_blocks, axis=3)
        gate = jnp.einsum('bhqd,bhnd->bhqn', qn, k_mean)
        past = jnp.arange(nb)[None, :] < q_blk[:, None]
        gate = jnp.where(past[None, None], gate, -jnp.inf)
        _, sel_idx = lax.top_k(gate, topk)
        sel_valid = sel_idx < q_blk[None, None, :, None]
    bi = jnp.arange(B)[:, None, None, None]
    hi = jnp.arange(H)[None, :, None, None]
    kpos = jnp.arange(MOBA_BLOCK)

    def chunk(c):
        start = c * Q_CHUNK
        q_c = lax.dynamic_slice_in_dim(qn, start, Q_CHUNK, axis=2)
        t_c = start + jnp.arange(Q_CHUNK)
        own = start // MOBA_BLOCK
        k_own = lax.dynamic_index_in_dim(k_blocks, own, axis=2, keepdims=False)
        v_own = lax.dynamic_index_in_dim(v_blocks, own, axis=2, keepdims=False)
        dist_own = (t_c[:, None] - (own * MOBA_BLOCK + kpos)[None, :]).astype(jnp.float32)
        logit_own = (jnp.einsum('bhqd,bhkd->bhqk', q_c, k_own) * scale
                     - slopes[None, :, None, None] * dist_own[None, None])
        logit_own = jnp.where((dist_own >= 0)[None, None], logit_own, NEG_INF)
        if topk == 0:
            p_own = jax.nn.softmax(logit_own, axis=-1)
            return jnp.einsum('bhqk,bhkd->bhqd', p_own, v_own)
        idx_c = lax.dynamic_slice_in_dim(sel_idx, start, Q_CHUNK, axis=2)
        valid_c = lax.dynamic_slice_in_dim(sel_valid, start, Q_CHUNK, axis=2)
        k_sel = k_blocks[bi, hi, idx_c]
        v_sel = v_blocks[bi, hi, idx_c]
        s_pos = idx_c[..., None] * MOBA_BLOCK + kpos
        dist_sel = (t_c[None, None, :, None, None] - s_pos).astype(jnp.float32)
        logit_sel = (jnp.einsum('bhqd,bhqjkd->bhqjk', q_c, k_sel) * scale
                     - slopes[None, :, None, None, None] * dist_sel)
        logit_sel = jnp.where(valid_c[..., None], logit_sel, NEG_INF)
        n_sel = topk * MOBA_BLOCK
        logits = jnp.concatenate([logit_sel.reshape(B, H, Q_CHUNK, n_sel), logit_own], axis=-1)
        p = jax.nn.softmax(logits, axis=-1)
        p_sel = p[..., :n_sel].reshape(B, H, Q_CHUNK, topk, MOBA_BLOCK)
        p_own = p[..., n_sel:]
        return (jnp.einsum('bhqjk,bhqjkd->bhqd', p_sel, v_sel)
                + jnp.einsum('bhqk,bhkd->bhqd', p_own, v_own))

    outs = lax.map(chunk, jnp.arange(s_pad // Q_CHUNK))
    out = outs.transpose(1, 0, 3, 2, 4).reshape(B, s_pad, H * Dh)
    return out[:, :S]


def pool_attn_mixer(h, w_in, pool_w, pool_scale, q_gain, k_gain, w_out):
    B, S, _ = h.shape
    z = h @ w_in
    u, q, k, v = jnp.split(z, [POOL_WIDTH, POOL_WIDTH + ATTN_WIDTH, POOL_WIDTH + 2 * ATTN_WIDTH], axis=-1)
    p = multiscale_pool_minus_identity(u).astype(h.dtype)
    p = jnp.einsum('bsgc,gcd->bsgd', p.reshape(B, S, len(POOL_WINDOWS), POOL_GROUP), pool_w)
    p = p.reshape(B, S, POOL_WIDTH) * pool_scale
    a = moba_attention(q.reshape(B, S, ATTN_HEADS, HEAD_DIM), k.reshape(B, S, ATTN_HEADS, HEAD_DIM),
                       v.reshape(B, S, ATTN_HEADS, HEAD_DIM), q_gain, k_gain)
    return jnp.concatenate([p, a.astype(h.dtype)], axis=-1) @ w_out


def rglru_mixer(h, w_in, conv_w, conv_b, w_a, b_a, w_x, b_x, lam, w_out):
    B, S, _ = h.shape
    z = h @ w_in
    gate, xr = jnp.split(z, 2, axis=-1)
    xr = causal_dwconv(xr, conv_w, conv_b)
    xb = xr.reshape(B, S, RNN_BLOCKS, RNN_BLOCK)
    r = jax.nn.sigmoid((jnp.einsum('bsnc,ncd->bsnd', xb, w_a).reshape(B, S, RNN_WIDTH) + b_a).astype(jnp.float32))
    i = jax.nn.sigmoid((jnp.einsum('bsnc,ncd->bsnd', xb, w_x).reshape(B, S, RNN_WIDTH) + b_x).astype(jnp.float32))
    log_a = -RG_C * r * jax.nn.softplus(-lam.astype(jnp.float32))
    a = jnp.exp(log_a)
    b = jnp.sqrt(-jnp.expm1(2.0 * log_a)) * (i * xr.astype(jnp.float32))

    def combine(left, right):
        a1, b1 = left
        a2, b2 = right
        return a1 * a2, a2 * b1 + b2

    _, hs = lax.associative_scan(combine, (a, b), axis=1)
    y = jax.nn.gelu(gate) * hs.astype(gate.dtype)
    return y @ w_out


def conv_ffn(h, w_in, conv_w, conv_b, w_out):
    z = causal_dwconv(h @ w_in, conv_w, conv_b)
    u, g = jnp.split(z, 2, axis=-1)
    return (jax.nn.gelu(g) * u) @ w_out


def setup_inputs(seed: int = 0) -> dict:
    key = jax.random.key(seed)
    ks = jax.random.split(key, 24)
    f32 = jnp.float32
    nrm = lambda k, shape, s: jax.random.normal(k, shape, f32) * s
    u = jax.random.uniform(ks[16], (N_ODD, RNN_WIDTH), f32, minval=0.9, maxval=0.999)
    a0 = u ** (1.0 / RG_C)
    return {
        "x": nrm(ks[0], (BATCH, SEQ, D_MODEL), 1.0),
        "norm_mix": 1.0 + nrm(ks[1], (DEPTH, D_MODEL), 0.05),
        "norm_ffn": 1.0 + nrm(ks[2], (DEPTH, D_MODEL), 0.05),
        "pa_w_in": nrm(ks[3], (N_EVEN, D_MODEL, POOL_WIDTH + 3 * ATTN_WIDTH), D_MODEL ** -0.5),
        "pa_pool_w": nrm(ks[4], (N_EVEN, len(POOL_WINDOWS), POOL_GROUP, POOL_GROUP), POOL_GROUP ** -0.5),
        "pa_pool_scale": 1.0 + nrm(ks[5], (N_EVEN, POOL_WIDTH), 0.1),
        "pa_q_gain": 1.0 + nrm(ks[6], (N_EVEN, HEAD_DIM), 0.05),
        "pa_k_gain": 1.0 + nrm(ks[7], (N_EVEN, HEAD_DIM), 0.05),
        "pa_w_out": nrm(ks[8], (N_EVEN, POOL_WIDTH + ATTN_WIDTH, D_MODEL), (POOL_WIDTH + ATTN_WIDTH) ** -0.5),
        "rg_w_in": nrm(ks[9], (N_ODD, D_MODEL, 2 * RNN_WIDTH), D_MODEL ** -0.5),
        "rg_conv_w": nrm(ks[10], (N_ODD, RNN_CONV, RNN_WIDTH), RNN_CONV ** -0.5),
        "rg_conv_b": nrm(ks[11], (N_ODD, RNN_WIDTH), 0.01),
        "rg_w_a": nrm(ks[12], (N_ODD, RNN_BLOCKS, RNN_BLOCK, RNN_BLOCK), RNN_BLOCK ** -0.5),
        "rg_b_a": nrm(ks[13], (N_ODD, RNN_WIDTH), 0.01),
        "rg_w_x": nrm(ks[14], (N_ODD, RNN_BLOCKS, RNN_BLOCK, RNN_BLOCK), RNN_BLOCK ** -0.5),
        "rg_b_x": nrm(ks[15], (N_ODD, RNN_WIDTH), 0.01),
        "rg_lambda": jnp.log(a0) - jnp.log1p(-a0),
        "rg_w_out": nrm(ks[17], (N_ODD, RNN_WIDTH, D_MODEL), RNN_WIDTH ** -0.5),
        "ffn_w_in": nrm(ks[18], (DEPTH, D_MODEL, 2 * D_FF), D_MODEL ** -0.5),
        "ffn_conv_w": nrm(ks[19], (DEPTH, FFN_CONV, 2 * D_FF), FFN_CONV ** -0.5),
        "ffn_conv_b": nrm(ks[20], (DEPTH, 2 * D_FF), 0.01),
        "ffn_w_out": nrm(ks[21], (DEPTH, D_FF, D_MODEL), D_FF ** -0.5),
    }


def reference(x, norm_mix, norm_ffn, pa_w_in, pa_pool_w, pa_pool_scale, pa_q_gain, pa_k_gain,
              pa_w_out, rg_w_in, rg_conv_w, rg_conv_b, rg_w_a, rg_b_a, rg_w_x, rg_b_x, rg_lambda,
              rg_w_out, ffn_w_in, ffn_conv_w, ffn_conv_b, ffn_w_out):
    for l in range(DEPTH):
        j = l // 2
        h = rms_norm(x, norm_mix[l])
        if l % 2 == 0:
            x = x + pool_attn_mixer(h, pa_w_in[j], pa_pool_w[j], pa_pool_scale[j],
                                    pa_q_gain[j], pa_k_gain[j], pa_w_out[j])
        else:
            x = x + rglru_mixer(h, rg_w_in[j], rg_conv_w[j], rg_conv_b[j], rg_w_a[j], rg_b_a[j],
                                rg_w_x[j], rg_b_x[j], rg_lambda[j], rg_w_out[j])
        h = rms_norm(x, norm_ffn[l])
        x = x + conv_ffn(h, ffn_w_in[l], ffn_conv_w[l], ffn_conv_b[l], ffn_w_out[l])
    return x
```

```python
import functools

import jax
import jax.numpy as jnp
from jax import lax
from jax.experimental import pallas as pl
from jax.experimental.pallas import tpu as pltpu

F32 = jnp.float32
BF16 = jnp.bfloat16

EPS = 1e-6
NEG_INF = -1e30
POOL_WINDOWS = (2, 4, 8, 16)
POOL_GROUP = 128
HEAD_DIM = 64
MOBA_BLOCK = 256
MOBA_TOPK = 3
RG_C = 8.0
RNN_BLOCK = 128

V7X_LANES = 128
V7X_SUBLANES = 8
V7X_VMEM_LIMIT_BYTES = 56 * 1024 * 1024

SEQ_TILE = 512
FFN_CHUNK = 256


def _rms(x, g):
    ms = jnp.mean(x * x, axis=-1, keepdims=True)
    return x * lax.rsqrt(ms + EPS) * g


def _const_spec(shape):
    zeros = (0,) * len(shape)
    return pl.BlockSpec(shape, lambda *_: zeros, pipeline_mode=pl.Buffered(1))


def _params(sem):
    return pltpu.CompilerParams(dimension_semantics=sem, vmem_limit_bytes=V7X_VMEM_LIMIT_BYTES)


def _ffn_kernel(x_ref, g_ref, wug_ref, cw_ref, cb_ref, wo_ref, o_ref, h_scr, acc_scr, carry_scr,
                *, nch, cw):
    ts = x_ref.shape[1]
    x = x_ref[0]
    h_scr[...] = _rms(x, g_ref[...]).astype(BF16)

    @pl.when(pl.program_id(1) == 0)
    def _():
        carry_scr[...] = jnp.zeros_like(carry_scr)

    acc_scr[...] = jnp.zeros_like(acc_scr)

    def chunk(c, carry):
        z = jnp.dot(h_scr[...], wug_ref[c], preferred_element_type=F32)
        prev = carry_scr[c]
        carry_scr[c] = z[ts - V7X_SUBLANES:, :]
        ext = jnp.concatenate([prev, z], axis=0)
        z1 = ext[V7X_SUBLANES - 1:V7X_SUBLANES - 1 + ts]
        z2 = ext[V7X_SUBLANES - 2:V7X_SUBLANES - 2 + ts]
        w = cw_ref[c]
        y = z2 * w[0:1] + z1 * w[1:2] + z * w[2:3] + cb_ref[c]
        act = (jax.nn.gelu(y[:, cw:]) * y[:, :cw]).astype(BF16)
        acc_scr[...] += jnp.dot(act, wo_ref[c], preferred_element_type=F32)
        return carry

    lax.fori_loop(0, nch, chunk, 0)
    o_ref[0] = x + acc_scr[...]


def _ffn(x, g, w_in, conv_w, conv_b, w_out):
    b, s, d = x.shape
    dff = w_out.shape[0]
    cw = FFN_CHUNK
    nch = dff // cw
    ts = SEQ_TILE
    wu = w_in[:, :dff].reshape(d, nch, cw)
    wg = w_in[:, dff:].reshape(d, nch, cw)
    wug = jnp.concatenate([wu, wg], axis=-1).transpose(1, 0, 2).astype(BF16)
    cwu = conv_w[:, :dff].reshape(-1, nch, cw)
    cwg = conv_w[:, dff:].reshape(-1, nch, cw)
    cwr = jnp.concatenate([cwu, cwg], axis=-1).transpose(1, 0, 2)
    cbr = jnp.concatenate([conv_b[:dff].reshape(nch, 1, cw), conv_b[dff:].reshape(nch, 1, cw)], axis=-1)
    wo = w_out.reshape(nch, cw, d).astype(BF16)
    kern = functools.partial(_ffn_kernel, nch=nch, cw=cw)
    return pl.pallas_call(
        kern,
        out_shape=jax.ShapeDtypeStruct((b, s, d), F32),
        grid=(b, s // ts),
        in_specs=[
            pl.BlockSpec((1, ts, d), lambda i, j: (i, j, 0)),
            _const_spec((1, d)),
            _const_spec((nch, d, 2 * cw)),
            _const_spec((nch, conv_w.shape[0], 2 * cw)),
            _const_spec((nch, 1, 2 * cw)),
            _const_spec((nch, cw, d)),
        ],
        out_specs=pl.BlockSpec((1, ts, d), lambda i, j: (i, j, 0)),
        scratch_shapes=[
            pltpu.VMEM((ts, d), BF16),
            pltpu.VMEM((ts, d), F32),
            pltpu.VMEM((nch, V7X_SUBLANES, 2 * cw), F32),
        ],
        compiler_params=_params(("arbitrary", "arbitrary")),
        name="conv_ffn",
    )(x, g.reshape(1, d), wug, cwr, cbr, wo)


def _rglru_kernel(x_ref, g_ref, win_ref, cw_ref, cb_ref, wax_ref, ba_ref, bx_ref, lam_ref, wo_ref,
                  o_ref, xcarry_scr, hcarry_scr, *, width, nblk):
    ts = x_ref.shape[1]
    x = x_ref[0]
    h = _rms(x, g_ref[...]).astype(BF16)

    @pl.when(pl.program_id(1) == 0)
    def _():
        xcarry_scr[...] = jnp.zeros_like(xcarry_scr)
        hcarry_scr[...] = jnp.zeros_like(hcarry_scr)

    z = jnp.dot(h, win_ref[...], preferred_element_type=F32)
    gate = z[:, :width]
    xr = z[:, width:]
    ext = jnp.concatenate([xcarry_scr[...], xr], axis=0)
    xcarry_scr[...] = xr[ts - V7X_SUBLANES:, :]
    kw = cw_ref.shape[0]
    xc = cb_ref[...] + xr * cw_ref[kw - 1:kw, :]
    for k in range(kw - 1):
        off = V7X_SUBLANES - (kw - 1) + k
        xc = xc + ext[off:off + ts] * cw_ref[k:k + 1, :]

    xcb = xc.astype(BF16)
    ra, rx = [], []
    for n in range(nblk):
        ax = jnp.dot(xcb[:, n * RNN_BLOCK:(n + 1) * RNN_BLOCK], wax_ref[n], preferred_element_type=F32)
        ra.append(ax[:, :RNN_BLOCK])
        rx.append(ax[:, RNN_BLOCK:])
    r = jax.nn.sigmoid(jnp.concatenate(ra, axis=-1) + ba_ref[...])
    ig = jax.nn.sigmoid(jnp.concatenate(rx, axis=-1) + bx_ref[...])
    log_a = (-RG_C * jax.nn.softplus(-lam_ref[...])) * r
    a = jnp.exp(log_a)
    bb = jnp.sqrt(1.0 - a * a) * (ig * xc)

    row = lax.broadcasted_iota(jnp.int32, (ts, 1), 0)
    k = 1
    while k < ts:
        a_s = pltpu.roll(a, k, 0)
        b_s = pltpu.roll(bb, k, 0)
        m = row >= k
        bb = jnp.where(m, a * b_s + bb, bb)
        a = jnp.where(m, a * a_s, a)
        k *= 2
    hs = bb + a * hcarry_scr[...]
    hcarry_scr[...] = hs[ts - 1:ts, :]

    y = (jax.nn.gelu(gate) * hs).astype(BF16)
    o_ref[0] = x + jnp.dot(y, wo_ref[...], preferred_element_type=F32)


def _rglru(x, g, w_in, conv_w, conv_b, w_a, b_a, w_x, b_x, lam, w_out):
    b, s, d = x.shape
    width = w_out.shape[0]
    nblk = w_a.shape[0]
    ts = SEQ_TILE
    wax = jnp.concatenate([w_a, w_x], axis=-1).astype(BF16)
    kern = functools.partial(_rglru_kernel, width=width, nblk=nblk)
    return pl.pallas_call(
        kern,
        out_shape=jax.ShapeDtypeStruct((b, s, d), F32),
        grid=(b, s // ts),
        in_specs=[
            pl.BlockSpec((1, ts, d), lambda i, j: (i, j, 0)),
            _const_spec((1, d)),
            _const_spec((d, 2 * width)),
            _const_spec(conv_w.shape),
            _const_spec((1, width)),
            _const_spec(wax.shape),
            _const_spec((1, width)),
            _const_spec((1, width)),
            _const_spec((1, width)),
            _const_spec((width, d)),
        ],
        out_specs=pl.BlockSpec((1, ts, d), lambda i, j: (i, j, 0)),
        scratch_shapes=[
            pltpu.VMEM((V7X_SUBLANES, width), F32),
            pltpu.VMEM((1, width), F32),
        ],
        compiler_params=_params(("arbitrary", "arbitrary")),
        name="rglru_mixer",
    )(x, g.reshape(1, d), w_in.astype(BF16), conv_w, conv_b.reshape(1, width), wax,
      b_a.reshape(1, width), b_x.reshape(1, width), lam.reshape(1, width), w_out.astype(BF16))


def _pa_in_kernel(x_ref, g_ref, wu_ref, wk_ref, wqt_ref, wvt_ref, pw_ref, ps_ref, qg_ref, kg_ref, e_ref,
                  p_ref, qt_ref, k_ref, vt_ref, km_ref, ucarry_scr):
    ts = x_ref.shape[1]
    nhp = qt_ref.shape[1]
    nbt = ts // MOBA_BLOCK
    s = pl.program_id(1)
    h = _rms(x_ref[0], g_ref[...]).astype(BF16)

    @pl.when(s == 0)
    def _():
        ucarry_scr[...] = jnp.zeros_like(ucarry_scr)

    u = jnp.dot(h, wu_ref[...], preferred_element_type=F32)
    halo = ucarry_scr.shape[0]
    ext = jnp.concatenate([ucarry_scr[...], u], axis=0)
    ucarry_scr[...] = u[ts - halo:, :]
    pos = s * ts + lax.broadcasted_iota(jnp.int32, (ts, 1), 0)
    for gi, w in enumerate(POOL_WINDOWS):
        lanes = slice(gi * POOL_GROUP, (gi + 1) * POOL_GROUP)
        acc = ext[:, lanes]
        sh = 1
        while sh < w:
            acc = acc + pltpu.roll(acc, sh, 0)
            sh *= 2
        cnt = jnp.minimum(pos + 1, w).astype(F32)
        pg = acc[halo:] / cnt - u[:, lanes]
        out = jnp.dot(pg.astype(BF16), pw_ref[gi], preferred_element_type=F32) * ps_ref[:, lanes]
        p_ref[0, :, lanes] = out.astype(BF16)

    nt = (((1,), (1,)), ((), ()))
    nh = wqt_ref.shape[0] // HEAD_DIM
    qt = lax.dot_general(wqt_ref[...], h, nt, preferred_element_type=F32)
    q3 = qt.reshape(nh, HEAD_DIM, ts)
    ms = jnp.sum(q3 * q3, axis=1, keepdims=True) * (1.0 / HEAD_DIM)
    qn = ((q3 * lax.rsqrt(ms + EPS)).reshape(nh * HEAD_DIM, ts) * qg_ref[...]).astype(BF16)
    vt = lax.dot_general(wvt_ref[...], h, nt, preferred_element_type=F32).astype(BF16)

    kk = jnp.dot(h, wk_ref[...], preferred_element_type=F32)
    ssq = jnp.dot((kk * kk).astype(BF16), e_ref[...], preferred_element_type=F32)
    kn = kk * lax.rsqrt(ssq * (1.0 / HEAD_DIM) + EPS) * kg_ref[...]
    knb = kn.astype(BF16)

    hpw = 2 * HEAD_DIM
    for jb in range(nbt):
        cols = slice(jb * MOBA_BLOCK, (jb + 1) * MOBA_BLOCK)
        kmean = jnp.sum(kn[cols, :], axis=0, keepdims=True) * (1.0 / MOBA_BLOCK)
        for hp in range(nhp):
            rows = slice(hp * hpw, (hp + 1) * hpw)
            qt_ref[0, hp, jb] = qn[rows, cols]
            vt_ref[0, hp, jb] = vt[rows, cols]
            k_ref[0, hp, jb] = knb[cols, rows]
            km_ref[0, hp, pl.ds(s * nbt + jb, 1), :] = kmean[:, rows]


def _pa_in(x, g, w_in, pool_w, pool_scale, q_gain, k_gain):
    b, s, d = x.shape
    pw = pool_w.shape[0] * POOL_GROUP
    aw = (w_in.shape[1] - pw) // 3
    nh = aw // HEAD_DIM
    nhp = nh // 2
    nb = s // MOBA_BLOCK
    ts = SEQ_TILE
    hpw = 2 * HEAD_DIM
    wu = w_in[:, :pw].astype(BF16)
    wqt = w_in[:, pw:pw + aw].T.astype(BF16)
    wk = w_in[:, pw + aw:pw + 2 * aw].astype(BF16)
    wvt = w_in[:, pw + 2 * aw:].T.astype(BF16)
    qg = (jnp.tile(q_gain, nh) * (HEAD_DIM ** -0.5)).reshape(aw, 1)
    kg = jnp.tile(k_gain, nh).reshape(1, aw)
    hid = jnp.arange(aw) // HEAD_DIM
    e = (hid[:, None] == hid[None, :]).astype(BF16)
    return pl.pallas_call(
        _pa_in_kernel,
        out_shape=(
            jax.ShapeDtypeStruct((b, s, pw), BF16),
            jax.ShapeDtypeStruct((b, nhp, nb, hpw, MOBA_BLOCK), BF16),
            jax.ShapeDtypeStruct((b, nhp, nb, MOBA_BLOCK, hpw), BF16),
            jax.ShapeDtypeStruct((b, nhp, nb, hpw, MOBA_BLOCK), BF16),
            jax.ShapeDtypeStruct((b, nhp, nb, hpw), F32),
        ),
        grid=(b, s // ts),
        in_specs=[
            pl.BlockSpec((1, ts, d), lambda i, j: (i, j, 0)),
            _const_spec((1, d)),
            _const_spec((d, pw)),
            _const_spec((d, aw)),
            _const_spec((aw, d)),
            _const_spec((aw, d)),
            _const_spec(pool_w.shape),
            _const_spec((1, pw)),
            _const_spec((aw, 1)),
            _const_spec((1, aw)),
            _const_spec((aw, aw)),
        ],
        out_specs=(
            pl.BlockSpec((1, ts, pw), lambda i, j: (i, j, 0)),
            pl.BlockSpec((1, nhp, ts // MOBA_BLOCK, hpw, MOBA_BLOCK), lambda i, j: (i, 0, j, 0, 0)),
            pl.BlockSpec((1, nhp, ts // MOBA_BLOCK, MOBA_BLOCK, hpw), lambda i, j: (i, 0, j, 0, 0)),
            pl.BlockSpec((1, nhp, ts // MOBA_BLOCK, hpw, MOBA_BLOCK), lambda i, j: (i, 0, j, 0, 0)),
            pl.BlockSpec((1, nhp, nb, hpw), lambda i, j: (i, 0, 0, 0)),
        ),
        scratch_shapes=[pltpu.VMEM((max(POOL_WINDOWS), pw), F32)],
        compiler_params=_params(("arbitrary", "arbitrary")),
        name="pool_attn_in",
    )(x, g.reshape(1, d), wu, wk, wqt, wvt, pool_w.astype(BF16), pool_scale.reshape(1, pw), qg, kg, e)


def _moba_kernel(qt_ref, k_ref, vt_ref, km_ref, bias_ref, diag_ref, jsl_ref, o_ref, sb_scr):
    i = pl.program_id(2)
    nb = k_ref.shape[2]
    hpw, bq = qt_ref.shape[3], qt_ref.shape[4]
    qt2 = qt_ref[0, 0, 0]
    hrow = lax.broadcasted_iota(jnp.int32, (hpw, bq), 0) < HEAD_DIM
    jj = lax.broadcasted_iota(jnp.int32, (nb, bq), 0)
    kmb = km_ref[0, 0].astype(BF16)
    outs = []
    for hh in range(2):
        mine = hrow if hh == 0 else jnp.logical_not(hrow)
        qh = jnp.where(mine, qt2, jnp.zeros_like(qt2))

        gate = jnp.dot(kmb, qh, preferred_element_type=F32)
        gt = jnp.where(jj < i, gate, -jnp.inf)
        rank = jnp.zeros((nb, bq), jnp.int32)
        for m in range(nb):
            gm = gt[m:m + 1, :]
            ahead = (gm > gt) | ((gm == gt) & (jj > m))
            rank = rank + ahead.astype(jnp.int32)
        keep = ((jj < i) & (rank < MOBA_TOPK)) | (jj == i)
        sb_scr[hh] = jnp.where(keep, 0.0, NEG_INF) + jsl_ref[hh]

        def vsel(vt2, mine=mine):
            return jnp.where(mine, vt2, jnp.ones_like(vt2))

        st = jnp.dot(k_ref[0, 0, i], qh, preferred_element_type=F32)
        st = st + diag_ref[hh] + sb_scr[hh, pl.ds(i, 1), :]
        mx = jnp.max(st, axis=0, keepdims=True)
        p = jnp.exp(st - mx)
        acc = jnp.dot(vsel(vt_ref[0, 0, i]), p.astype(BF16), preferred_element_type=F32)

        def body(j, carry, qh=qh, hh=hh, vsel=vsel):
            mx, acc = carry
            st = jnp.dot(k_ref[0, 0, j], qh, preferred_element_type=F32)
            st = st + bias_ref[hh] + sb_scr[hh, pl.ds(j, 1), :]
            mn = jnp.maximum(mx, jnp.max(st, axis=0, keepdims=True))
            p = jnp.exp(st - mn)
            acc = acc * jnp.exp(mx - mn) + jnp.dot(vsel(vt_ref[0, 0, j]), p.astype(BF16),
                                                   preferred_element_type=F32)
            return mn, acc

        mx, acc = lax.fori_loop(0, i, body, (mx, acc))
        if hh == 0:
            outs.append(acc[:HEAD_DIM] / acc[HEAD_DIM:])
        else:
            outs.append(acc[HEAD_DIM:] / acc[:HEAD_DIM])
    ot = jnp.concatenate(outs, axis=0)
    o_ref[0] = ot.T.astype(BF16)


def _moba(qt, k, vt, km):
    b, nhp, nb, hpw, bq = qt.shape
    nh = 2 * nhp
    slopes = 2.0 ** (-8.0 * jnp.arange(1, nh + 1, dtype=F32) / nh)
    r = jnp.arange(MOBA_BLOCK, dtype=F32)
    bias = slopes[:, None, None] * jnp.broadcast_to(r[:, None], (MOBA_BLOCK, bq))[None]
    causal = jnp.arange(MOBA_BLOCK)[:, None] <= jnp.arange(bq)[None, :]
    diag = bias + jnp.where(causal, 0.0, NEG_INF)[None]
    jsl = slopes[:, None, None] * jnp.broadcast_to(
        (jnp.arange(nb, dtype=F32) * MOBA_BLOCK)[:, None], (nb, bq))[None]
    return pl.pallas_call(
        _moba_kernel,
        out_shape=jax.ShapeDtypeStruct((b, nb * bq, nhp * hpw), BF16),
        grid=(b, nhp, nb),
        in_specs=[
            pl.BlockSpec((1, 1, 1, hpw, bq), lambda bi, hp, i: (bi, hp, i, 0, 0)),
            pl.BlockSpec((1, 1, nb, MOBA_BLOCK, hpw), lambda bi, hp, i: (bi, hp, 0, 0, 0)),
            pl.BlockSpec((1, 1, nb, hpw, MOBA_BLOCK), lambda bi, hp, i: (bi, hp, 0, 0, 0)),
            pl.BlockSpec((1, 1, nb, hpw), lambda bi, hp, i: (bi, hp, 0, 0)),
            pl.BlockSpec((2, MOBA_BLOCK, bq), lambda bi, hp, i: (hp, 0, 0)),
            pl.BlockSpec((2, MOBA_BLOCK, bq), lambda bi, hp, i: (hp, 0, 0)),
            pl.BlockSpec((2, nb, bq), lambda bi, hp, i: (hp, 0, 0)),
        ],
        out_specs=pl.BlockSpec((1, bq, hpw), lambda bi, hp, i: (bi, i, hp)),
        scratch_shapes=[pltpu.VMEM((2, nb, bq), F32)],
        compiler_params=_params(("arbitrary", "arbitrary", "arbitrary")),
        name="moba_attention",
    )(qt, k, vt, km, bias, diag, jsl)


def _pa_out_kernel(x_ref, p_ref, a_ref, wp_ref, wa_ref, o_ref):
    y = jnp.dot(p_ref[0], wp_ref[...], preferred_element_type=F32)
    y = y + jnp.dot(a_ref[0], wa_ref[...], preferred_element_type=F32)
    o_ref[0] = x_ref[0] + y


def _pa_out(x, p, a, w_out):
    b, s, d = x.shape
    pw, aw = p.shape[-1], a.shape[-1]
    ts = SEQ_TILE
    return pl.pallas_call(
        _pa_out_kernel,
        out_shape=jax.ShapeDtypeStruct((b, s, d), F32),
        grid=(b, s // ts),
        in_specs=[
            pl.BlockSpec((1, ts, d), lambda i, j: (i, j, 0)),
            pl.BlockSpec((1, ts, pw), lambda i, j: (i, j, 0)),
            pl.BlockSpec((1, ts, aw), lambda i, j: (i, j, 0)),
            _const_spec((pw, d)),
            _const_spec((aw, d)),
        ],
        out_specs=pl.BlockSpec((1, ts, d), lambda i, j: (i, j, 0)),
        compiler_params=_params(("arbitrary", "arbitrary")),
        name="pool_attn_out",
    )(x, p, a, w_out[:pw].astype(BF16), w_out[pw:].astype(BF16))


@jax.jit
def kernel(x, norm_mix, norm_ffn, pa_w_in, pa_pool_w, pa_pool_scale, pa_q_gain, pa_k_gain, pa_w_out,
           rg_w_in, rg_conv_w, rg_conv_b, rg_w_a, rg_b_a, rg_w_x, rg_b_x, rg_lambda, rg_w_out,
           ffn_w_in, ffn_conv_w, ffn_conv_b, ffn_w_out):
    depth = norm_mix.shape[0]
    for l in range(depth):
        j = l // 2
        if l % 2 == 0:
            p, qt, k, vt, km = _pa_in(x, norm_mix[l], pa_w_in[j], pa_pool_w[j], pa_pool_scale[j],
                                      pa_q_gain[j], pa_k_gain[j])
            a = _moba(qt, k, vt, km)
            x = _pa_out(x, p, a, pa_w_out[j])
        else:
            x = _rglru(x, norm_mix[l], rg_w_in[j], rg_conv_w[j], rg_conv_b[j], rg_w_a[j], rg_b_a[j],
                       rg_w_x[j], rg_b_x[j], rg_lambda[j], rg_w_out[j])
        x = _ffn(x, norm_ffn[l], ffn_w_in[l], ffn_conv_w[l], ffn_conv_b[l], ffn_w_out[l])
    return x
```

```python
import functools

import jax
import jax.numpy as jnp
from jax import lax
from jax.experimental import pallas as pl
from jax.experimental.pallas import tpu as pltpu

F32 = jnp.float32
BF16 = jnp.bfloat16

EPS = 1e-6
NEG_INF = -1e30
POOL_WINDOWS = (2, 4, 8, 16)
POOL_GROUP = 128
HEAD_DIM = 64
MOBA_BLOCK = 256
MOBA_TOPK = 3
RG_C = 8.0
RNN_BLOCK = 128

V7X_LANES = 128
V7X_SUBLANES = 8
V7X_VMEM_LIMIT_BYTES = 56 * 1024 * 1024

SEQ_TILE = 512
FFN_CHUNK = 256
QUERY_SPLIT = 128


def _rms(x, g):
    ms = jnp.mean(x * x, axis=-1, keepdims=True)
    return x * lax.rsqrt(ms + EPS) * g


def _const_spec(shape):
    zeros = (0,) * len(shape)
    return pl.BlockSpec(shape, lambda *_: zeros, pipeline_mode=pl.Buffered(1))


def _params(sem):
    return pltpu.CompilerParams(dimension_semantics=sem, vmem_limit_bytes=V7X_VMEM_LIMIT_BYTES)


def _ffn_kernel(x_ref, g_ref, wug_ref, cw_ref, cb_ref, wo_ref, o_ref, h_scr, acc_scr, carry_scr,
                *, nch, cw):
    ts = x_ref.shape[1]
    x = x_ref[0]
    h_scr[...] = _rms(x, g_ref[...]).astype(BF16)

    @pl.when(pl.program_id(1) == 0)
    def _():
        carry_scr[...] = jnp.zeros_like(carry_scr)

    acc_scr[...] = jnp.zeros_like(acc_scr)

    def chunk(c, carry):
        z = jnp.dot(h_scr[...], wug_ref[c], preferred_element_type=F32)
        prev = carry_scr[c]
        carry_scr[c] = z[ts - V7X_SUBLANES:, :]
        ext = jnp.concatenate([prev, z], axis=0)
        z1 = ext[V7X_SUBLANES - 1:V7X_SUBLANES - 1 + ts]
        z2 = ext[V7X_SUBLANES - 2:V7X_SUBLANES - 2 + ts]
        w = cw_ref[c]
        y = z2 * w[0:1] + z1 * w[1:2] + z * w[2:3] + cb_ref[c]
        act = (jax.nn.gelu(y[:, cw:]) * y[:, :cw]).astype(BF16)
        acc_scr[...] += jnp.dot(act, wo_ref[c], preferred_element_type=F32)
        return carry

    lax.fori_loop(0, nch, chunk, 0)
    o_ref[0] = x + acc_scr[...]


def _ffn(x, g, w_in, conv_w, conv_b, w_out):
    b, s, d = x.shape
    dff = w_out.shape[0]
    cw = FFN_CHUNK
    nch = dff // cw
    ts = SEQ_TILE
    wu = w_in[:, :dff].reshape(d, nch, cw)
    wg = w_in[:, dff:].reshape(d, nch, cw)
    wug = jnp.concatenate([wu, wg], axis=-1).transpose(1, 0, 2).astype(BF16)
    cwu = conv_w[:, :dff].reshape(-1, nch, cw)
    cwg = conv_w[:, dff:].reshape(-1, nch, cw)
    cwr = jnp.concatenate([cwu, cwg], axis=-1).transpose(1, 0, 2)
    cbr = jnp.concatenate([conv_b[:dff].reshape(nch, 1, cw), conv_b[dff:].reshape(nch, 1, cw)], axis=-1)
    wo = w_out.reshape(nch, cw, d).astype(BF16)
    kern = functools.partial(_ffn_kernel, nch=nch, cw=cw)
    return pl.pallas_call(
        kern,
        out_shape=jax.ShapeDtypeStruct((b, s, d), F32),
        grid=(b, s // ts),
        in_specs=[
            pl.BlockSpec((1, ts, d), lambda i, j: (i, j, 0)),
            _const_spec((1, d)),
            _const_spec((nch, d, 2 * cw)),
            _const_spec((nch, conv_w.shape[0], 2 * cw)),
            _const_spec((nch, 1, 2 * cw)),
            _const_spec((nch, cw, d)),
        ],
        out_specs=pl.BlockSpec((1, ts, d), lambda i, j: (i, j, 0)),
        scratch_shapes=[
            pltpu.VMEM((ts, d), BF16),
            pltpu.VMEM((ts, d), F32),
            pltpu.VMEM((nch, V7X_SUBLANES, 2 * cw), F32),
        ],
        compiler_params=_params(("arbitrary", "arbitrary")),
        name="conv_ffn",
    )(x, g.reshape(1, d), wug, cwr, cbr, wo)


def _rglru_kernel(x_ref, g_ref, win_ref, cw_ref, cb_ref, wax_ref, ba_ref, bx_ref, lam_ref, wo_ref,
                  o_ref, xcarry_scr, hcarry_scr, *, width, nblk):
    ts = x_ref.shape[1]
    x = x_ref[0]
    h = _rms(x, g_ref[...]).astype(BF16)

    @pl.when(pl.program_id(1) == 0)
    def _():
        xcarry_scr[...] = jnp.zeros_like(xcarry_scr)
        hcarry_scr[...] = jnp.zeros_like(hcarry_scr)

    z = jnp.dot(h, win_ref[...], preferred_element_type=F32)
    gate = z[:, :width]
    xr = z[:, width:]
    ext = jnp.concatenate([xcarry_scr[...], xr], axis=0)
    xcarry_scr[...] = xr[ts - V7X_SUBLANES:, :]
    kw = cw_ref.shape[0]
    xc = cb_ref[...] + xr * cw_ref[kw - 1:kw, :]
    for k in range(kw - 1):
        off = V7X_SUBLANES - (kw - 1) + k
        xc = xc + ext[off:off + ts] * cw_ref[k:k + 1, :]

    xcb = xc.astype(BF16)
    ra, rx = [], []
    for n in range(nblk):
        ax = jnp.dot(xcb[:, n * RNN_BLOCK:(n + 1) * RNN_BLOCK], wax_ref[n], preferred_element_type=F32)
        ra.append(ax[:, :RNN_BLOCK])
        rx.append(ax[:, RNN_BLOCK:])
    r = jax.nn.sigmoid(jnp.concatenate(ra, axis=-1) + ba_ref[...])
    ig = jax.nn.sigmoid(jnp.concatenate(rx, axis=-1) + bx_ref[...])
    log_a = (-RG_C * jax.nn.softplus(-lam_ref[...])) * r
    a = jnp.exp(log_a)
    bb = jnp.sqrt(1.0 - a * a) * (ig * xc)

    row = lax.broadcasted_iota(jnp.int32, (ts, 1), 0)
    k = 1
    while k < ts:
        a_s = pltpu.roll(a, k, 0)
        b_s = pltpu.roll(bb, k, 0)
        m = row >= k
        bb = jnp.where(m, a * b_s + bb, bb)
        a = jnp.where(m, a * a_s, a)
        k *= 2
    hs = bb + a * hcarry_scr[...]
    hcarry_scr[...] = hs[ts - 1:ts, :]

    y = (jax.nn.gelu(gate) * hs).astype(BF16)
    o_ref[0] = x + jnp.dot(y, wo_ref[...], preferred_element_type=F32)


def _rglru(x, g, w_in, conv_w, conv_b, w_a, b_a, w_x, b_x, lam, w_out):
    b, s, d = x.shape
    width = w_out.shape[0]
    nblk = w_a.shape[0]
    ts = SEQ_TILE
    wax = jnp.concatenate([w_a, w_x], axis=-1).astype(BF16)
    kern = functools.partial(_rglru_kernel, width=width, nblk=nblk)
    return pl.pallas_call(
        kern,
        out_shape=jax.ShapeDtypeStruct((b, s, d), F32),
        grid=(b, s // ts),
        in_specs=[
            pl.BlockSpec((1, ts, d), lambda i, j: (i, j, 0)),
            _const_spec((1, d)),
            _const_spec((d, 2 * width)),
            _const_spec(conv_w.shape),
            _const_spec((1, width)),
            _const_spec(wax.shape),
            _const_spec((1, width)),
            _const_spec((1, width)),
            _const_spec((1, width)),
            _const_spec((width, d)),
        ],
        out_specs=pl.BlockSpec((1, ts, d), lambda i, j: (i, j, 0)),
        scratch_shapes=[
            pltpu.VMEM((V7X_SUBLANES, width), F32),
            pltpu.VMEM((1, width), F32),
        ],
        compiler_params=_params(("arbitrary", "arbitrary")),
        name="rglru_mixer",
    )(x, g.reshape(1, d), w_in.astype(BF16), conv_w, conv_b.reshape(1, width), wax,
      b_a.reshape(1, width), b_x.reshape(1, width), lam.reshape(1, width), w_out.astype(BF16))


def _pa_in_kernel(x_ref, g_ref, wu_ref, wk_ref, wqt_ref, wvt_ref, pw_ref, ps_ref, qg_ref, kg_ref, e_ref,
                  p_ref, qt_ref, k_ref, vt_ref, km_ref, ucarry_scr):
    ts = x_ref.shape[1]
    nhp = qt_ref.shape[1]
    nbt = ts // MOBA_BLOCK
    s = pl.program_id(1)
    h = _rms(x_ref[0], g_ref[...]).astype(BF16)

    @pl.when(s == 0)
    def _():
        ucarry_scr[...] = jnp.zeros_like(ucarry_scr)

    u = jnp.dot(h, wu_ref[...], preferred_element_type=F32)
    halo = ucarry_scr.shape[0]
    ext = jnp.concatenate([ucarry_scr[...], u], axis=0)
    ucarry_scr[...] = u[ts - halo:, :]
    pos = s * ts + lax.broadcasted_iota(jnp.int32, (ts, 1), 0)
    for gi, w in enumerate(POOL_WINDOWS):
        lanes = slice(gi * POOL_GROUP, (gi + 1) * POOL_GROUP)
        acc = ext[:, lanes]
        sh = 1
        while sh < w:
            acc = acc + pltpu.roll(acc, sh, 0)
            sh *= 2
        cnt = jnp.minimum(pos + 1, w).astype(F32)
        pg = acc[halo:] / cnt - u[:, lanes]
        out = jnp.dot(pg.astype(BF16), pw_ref[gi], preferred_element_type=F32) * ps_ref[:, lanes]
        p_ref[0, :, lanes] = out.astype(BF16)

    nt = (((1,), (1,)), ((), ()))
    nh = wqt_ref.shape[0] // HEAD_DIM
    qt = lax.dot_general(wqt_ref[...], h, nt, preferred_element_type=F32)
    q3 = qt.reshape(nh, HEAD_DIM, ts)
    ms = jnp.sum(q3 * q3, axis=1, keepdims=True) * (1.0 / HEAD_DIM)
    qn = ((q3 * lax.rsqrt(ms + EPS)).reshape(nh * HEAD_DIM, ts) * qg_ref[...]).astype(BF16)
    vt = lax.dot_general(wvt_ref[...], h, nt, preferred_element_type=F32).astype(BF16)

    kk = jnp.dot(h, wk_ref[...], preferred_element_type=F32)
    ssq = jnp.dot((kk * kk).astype(BF16), e_ref[...], preferred_element_type=F32)
    kn = kk * lax.rsqrt(ssq * (1.0 / HEAD_DIM) + EPS) * kg_ref[...]
    knb = kn.astype(BF16)

    hpw = 2 * HEAD_DIM
    for jb in range(nbt):
        cols = slice(jb * MOBA_BLOCK, (jb + 1) * MOBA_BLOCK)
        kmean = jnp.sum(kn[cols, :], axis=0, keepdims=True) * (1.0 / MOBA_BLOCK)
        for hp in range(nhp):
            rows = slice(hp * hpw, (hp + 1) * hpw)
            qt_ref[0, hp, jb] = qn[rows, cols]
            vt_ref[0, hp, jb] = vt[rows, cols]
            k_ref[0, hp, jb] = knb[cols, rows]
            km_ref[0, hp, pl.ds(s * nbt + jb, 1), :] = kmean[:, rows]


def _pa_in(x, g, w_in, pool_w, pool_scale, q_gain, k_gain):
    b, s, d = x.shape
    pw = pool_w.shape[0] * POOL_GROUP
    aw = (w_in.shape[1] - pw) // 3
    nh = aw // HEAD_DIM
    nhp = nh // 2
    nb = s // MOBA_BLOCK
    ts = SEQ_TILE
    hpw = 2 * HEAD_DIM
    wu = w_in[:, :pw].astype(BF16)
    wqt = w_in[:, pw:pw + aw].T.astype(BF16)
    wk = w_in[:, pw + aw:pw + 2 * aw].astype(BF16)
    wvt = w_in[:, pw + 2 * aw:].T.astype(BF16)
    qg = (jnp.tile(q_gain, nh) * (HEAD_DIM ** -0.5)).reshape(aw, 1)
    kg = jnp.tile(k_gain, nh).reshape(1, aw)
    hid = jnp.arange(aw) // HEAD_DIM
    e = (hid[:, None] == hid[None, :]).astype(BF16)
    return pl.pallas_call(
        _pa_in_kernel,
        out_shape=(
            jax.ShapeDtypeStruct((b, s, pw), BF16),
            jax.ShapeDtypeStruct((b, nhp, nb, hpw, MOBA_BLOCK), BF16),
            jax.ShapeDtypeStruct((b, nhp, nb, MOBA_BLOCK, hpw), BF16),
            jax.ShapeDtypeStruct((b, nhp, nb, hpw, MOBA_BLOCK), BF16),
            jax.ShapeDtypeStruct((b, nhp, nb, hpw), F32),
        ),
        grid=(b, s // ts),
        in_specs=[
            pl.BlockSpec((1, ts, d), lambda i, j: (i, j, 0)),
            _const_spec((1, d)),
            _const_spec((d, pw)),
            _const_spec((d, aw)),
            _const_spec((aw, d)),
            _const_spec((aw, d)),
            _const_spec(pool_w.shape),
            _const_spec((1, pw)),
            _const_spec((aw, 1)),
            _const_spec((1, aw)),
            _const_spec((aw, aw)),
        ],
        out_specs=(
            pl.BlockSpec((1, ts, pw), lambda i, j: (i, j, 0)),
            pl.BlockSpec((1, nhp, ts // MOBA_BLOCK, hpw, MOBA_BLOCK), lambda i, j: (i, 0, j, 0, 0)),
            pl.BlockSpec((1, nhp, ts // MOBA_BLOCK, MOBA_BLOCK, hpw), lambda i, j: (i, 0, j, 0, 0)),
            pl.BlockSpec((1, nhp, ts // MOBA_BLOCK, hpw, MOBA_BLOCK), lambda i, j: (i, 0, j, 0, 0)),
            pl.BlockSpec((1, nhp, nb, hpw), lambda i, j: (i, 0, 0, 0)),
        ),
        scratch_shapes=[pltpu.VMEM((max(POOL_WINDOWS), pw), F32)],
        compiler_params=_params(("arbitrary", "arbitrary")),
        name="pool_attn_in",
    )(x, g.reshape(1, d), wu, wk, wqt, wvt, pool_w.astype(BF16), pool_scale.reshape(1, pw), qg, kg, e)


def _moba_kernel(qt_ref, k_ref, vt_ref, km_ref, bias_ref, jsl_ref, o_ref, pv_scr, m_scr, w_scr):
    i = pl.program_id(2)
    nb = k_ref.shape[2]
    hpw, bq = qt_ref.shape[3], qt_ref.shape[4]
    qt2 = qt_ref[0, 0, 0]
    hrow = lax.broadcasted_iota(jnp.int32, (hpw, bq), 0) < HEAD_DIM
    mine = (hrow, jnp.logical_not(hrow))
    jj = lax.broadcasted_iota(jnp.int32, (nb, bq), 0)
    kmb = km_ref[0, 0].astype(BF16)
    qh = [jnp.where(mine[hh], qt2, jnp.zeros_like(qt2)) for hh in range(2)]
    m_scr[...] = jnp.zeros_like(m_scr)

    for hh in range(2):
        gate = jnp.dot(kmb, qh[hh], preferred_element_type=F32)
        gt = jnp.where(jj < i, gate, -jnp.inf)
        rank = jnp.zeros((nb, bq), jnp.int32)
        for m in range(nb):
            gm = gt[m:m + 1, :]
            ahead = (gm > gt) | ((gm == gt) & (jj > m))
            rank = rank + ahead.astype(jnp.int32)
        keep = ((jj < i) & (rank < MOBA_TOPK)) | (jj == i)
        w_scr[hh] = jnp.where(keep, 0.0, NEG_INF) + jsl_ref[hh]

    def block(j):
        kind = (j == i).astype(jnp.int32)
        kj = k_ref[0, 0, j]
        vj = vt_ref[0, 0, j]
        for hh in range(2):
            vh = jnp.where(mine[hh], vj, jnp.ones_like(vj))
            for qs in range(0, bq, QUERY_SPLIT):
                ql = slice(qs, qs + QUERY_SPLIT)
                st = jnp.dot(kj, qh[hh][:, ql], preferred_element_type=F32) + bias_ref[kind, hh, :, ql]
                mx = jnp.max(st, axis=0, keepdims=True)
                p = jnp.exp(st - mx).astype(BF16)
                pv_scr[j, hh, :, ql] = jnp.dot(vh, p, preferred_element_type=F32)
                m_scr[hh, qs // QUERY_SPLIT, pl.ds(j, 1), :] = mx

    def pair(t, carry):
        block(2 * t)
        block(jnp.minimum(2 * t + 1, nb - 1))
        return carry

    lax.fori_loop(0, (i + 2) // 2, pair, 0)

    for hh in range(2):
        mr = jnp.concatenate([m_scr[hh, t] for t in range(bq // QUERY_SPLIT)], axis=-1) + w_scr[hh]
        w_scr[hh] = jnp.exp(mr - jnp.max(mr, axis=0, keepdims=True))

    def comb(j, acc, hh):
        return acc + pv_scr[j, hh] * w_scr[hh, pl.ds(j, 1), :]

    zero = jnp.zeros((hpw, bq), F32)
    acc0 = lax.fori_loop(0, i + 1, functools.partial(comb, hh=0), zero)
    acc1 = lax.fori_loop(0, i + 1, functools.partial(comb, hh=1), zero)
    ot = jnp.concatenate([acc0[:HEAD_DIM] / acc0[HEAD_DIM:], acc1[HEAD_DIM:] / acc1[:HEAD_DIM]], axis=0)
    o_ref[0] = ot.T.astype(BF16)


def _moba(qt, k, vt, km):
    b, nhp, nb, hpw, bq = qt.shape
    nh = 2 * nhp
    slopes = 2.0 ** (-8.0 * jnp.arange(1, nh + 1, dtype=F32) / nh)
    r = jnp.arange(MOBA_BLOCK, dtype=F32)
    past = slopes[:, None, None] * jnp.broadcast_to(r[:, None], (MOBA_BLOCK, bq))[None]
    causal = jnp.arange(MOBA_BLOCK)[:, None] <= jnp.arange(bq)[None, :]
    bias = jnp.stack([past, past + jnp.where(causal, 0.0, NEG_INF)[None]])
    jsl = slopes[:, None, None] * jnp.broadcast_to(
        (jnp.arange(nb, dtype=F32) * MOBA_BLOCK)[:, None], (nb, bq))[None]
    return pl.pallas_call(
        _moba_kernel,
        out_shape=jax.ShapeDtypeStruct((b, nb * bq, nhp * hpw), BF16),
        grid=(b, nhp, nb),
        in_specs=[
            pl.BlockSpec((1, 1, 1, hpw, bq), lambda bi, hp, i: (bi, hp, i, 0, 0)),
            pl.BlockSpec((1, 1, nb, MOBA_BLOCK, hpw), lambda bi, hp, i: (bi, hp, 0, 0, 0)),
            pl.BlockSpec((1, 1, nb, hpw, MOBA_BLOCK), lambda bi, hp, i: (bi, hp, 0, 0, 0)),
            pl.BlockSpec((1, 1, nb, hpw), lambda bi, hp, i: (bi, hp, 0, 0)),
            pl.BlockSpec((2, 2, MOBA_BLOCK, bq), lambda bi, hp, i: (0, hp, 0, 0)),
            pl.BlockSpec((2, nb, bq), lambda bi, hp, i: (hp, 0, 0)),
        ],
        out_specs=pl.BlockSpec((1, bq, hpw), lambda bi, hp, i: (bi, i, hp)),
        scratch_shapes=[
            pltpu.VMEM((nb, 2, hpw, bq), F32),
            pltpu.VMEM((2, bq // QUERY_SPLIT, nb, QUERY_SPLIT), F32),
            pltpu.VMEM((2, nb, bq), F32),
        ],
        compiler_params=_params(("arbitrary", "arbitrary", "arbitrary")),
        name="moba_attention",
    )(qt, k, vt, km, bias, jsl)


def _pa_out_kernel(x_ref, p_ref, a_ref, wp_ref, wa_ref, o_ref):
    y = jnp.dot(p_ref[0], wp_ref[...], preferred_element_type=F32)
    y = y + jnp.dot(a_ref[0], wa_ref[...], preferred_element_type=F32)
    o_ref[0] = x_ref[0] + y


def _pa_out(x, p, a, w_out):
    b, s, d = x.shape
    pw, aw = p.shape[-1], a.shape[-1]
    ts = SEQ_TILE
    return pl.pallas_call(
        _pa_out_kernel,
        out_shape=jax.ShapeDtypeStruct((b, s, d), F32),
        grid=(b, s // ts),
        in_specs=[
            pl.BlockSpec((1, ts, d), lambda i, j: (i, j, 0)),
            pl.BlockSpec((1, ts, pw), lambda i, j: (i, j, 0)),
            pl.BlockSpec((1, ts, aw), lambda i, j: (i, j, 0)),
            _const_spec((pw, d)),
            _const_spec((aw, d)),
        ],
        out_specs=pl.BlockSpec((1, ts, d), lambda i, j: (i, j, 0)),
        compiler_params=_params(("arbitrary", "arbitrary")),
        name="pool_attn_out",
    )(x, p, a, w_out[:pw].astype(BF16), w_out[pw:].astype(BF16))


@jax.jit
def kernel(x, norm_mix, norm_ffn, pa_w_in, pa_pool_w, pa_pool_scale, pa_q_gain, pa_k_gain, pa_w_out,
           rg_w_in, rg_conv_w, rg_conv_b, rg_w_a, rg_b_a, rg_w_x, rg_b_x, rg_lambda, rg_w_out,
           ffn_w_in, ffn_conv_w, ffn_conv_b, ffn_w_out):
    depth = norm_mix.shape[0]
    for l in range(depth):
        j = l // 2
        if l % 2 == 0:
            p, qt, k, vt, km = _pa_in(x, norm_mix[l], pa_w_in[j], pa_pool_w[j], pa_pool_scale[j],
                                      pa_q_gain[j], pa_k_gain[j])
            a = _moba(qt, k, vt, km)
            x = _pa_out(x, p, a, pa_w_out[j])
        else:
            x = _rglru(x, norm_mix[l], rg_w_in[j], rg_conv_w[j], rg_conv_b[j], rg_w_a[j], rg_b_a[j],
                       rg_w_x[j], rg_b_x[j], rg_lambda[j], rg_w_out[j])
        x = _ffn(x, norm_ffn[l], ffn_w_in[l], ffn_conv_w[l], ffn_conv_b[l], ffn_w_out[l])
    return x
```

```python
import functools

import jax
import jax.numpy as jnp
from jax import lax
from jax.experimental import pallas as pl
from jax.experimental.pallas import tpu as pltpu

F32 = jnp.float32
BF16 = jnp.bfloat16

EPS = 1e-6
NEG_INF = -1e30
POOL_WINDOWS = (2, 4, 8, 16)
POOL_GROUP = 128
HEAD_DIM = 64
MOBA_BLOCK = 256
MOBA_TOPK = 3
RG_C = 8.0
RNN_BLOCK = 128

V7X_LANES = 128
V7X_SUBLANES = 8
V7X_VMEM_LIMIT_BYTES = 56 * 1024 * 1024

SEQ_TILE = 512
FFN_CHUNK = 256
QUERY_SPLIT = 128
Q_BLOCKS = 2
KEY_BLOCKS_PER_STEP = 2


def _rms(x, g):
    ms = jnp.mean(x * x, axis=-1, keepdims=True)
    return x * lax.rsqrt(ms + EPS) * g


def _const_spec(shape):
    zeros = (0,) * len(shape)
    return pl.BlockSpec(shape, lambda *_: zeros, pipeline_mode=pl.Buffered(1))


def _params(sem):
    return pltpu.CompilerParams(dimension_semantics=sem, vmem_limit_bytes=V7X_VMEM_LIMIT_BYTES)


def _ffn_kernel(x_ref, g_ref, wug_ref, cw_ref, cb_ref, wo_ref, o_ref, h_scr, carry_scr, z0_scr, z1_scr,
                *, nch, cw):
    z_scr = (z0_scr, z1_scr)
    ts = x_ref.shape[1]
    kw = cw_ref.shape[1]
    hl = V7X_SUBLANES
    h_scr[...] = _rms(x_ref[0], g_ref[...]).astype(BF16)
    o_ref[0] = x_ref[0]

    @pl.when(pl.program_id(1) == 0)
    def _():
        carry_scr[...] = jnp.zeros_like(carry_scr)

    def up(c, slot):
        z_scr[slot][hl:, :] = jnp.dot(h_scr[...], wug_ref[c], preferred_element_type=F32)

    def down(c, slot):
        zb = z_scr[slot]
        zb[0:hl, :] = carry_scr[c]
        carry_scr[c] = zb[ts:ts + hl, :]
        w = cw_ref[c]
        y = cb_ref[c] + zb[hl:, :] * w[kw - 1:kw]
        for k in range(kw - 1):
            off = hl - (kw - 1) + k
            y = y + zb[off:off + ts, :] * w[k:k + 1]
        a = (jax.nn.gelu(y[:, cw:]) * y[:, :cw]).astype(BF16)
        o_ref[0] += jnp.dot(a, wo_ref[c], preferred_element_type=F32)

    up(0, 0)

    def pair(q, carry):
        c = 2 * q
        up(c + 1, 1)
        down(c, 0)
        up(c + 2, 0)
        down(c + 1, 1)
        return carry

    npairs = (nch - 1) // 2
    lax.fori_loop(0, npairs, pair, 0)
    if nch % 2 == 0:
        up(nch - 1, 1)
        down(nch - 2, 0)
        down(nch - 1, 1)
    else:
        down(nch - 1, 0)


def _ffn(x, g, w_in, conv_w, conv_b, w_out):
    b, s, d = x.shape
    dff = w_out.shape[0]
    cw = FFN_CHUNK
    nch = dff // cw
    ts = SEQ_TILE
    wu = w_in[:, :dff].reshape(d, nch, cw)
    wg = w_in[:, dff:].reshape(d, nch, cw)
    wug = jnp.concatenate([wu, wg], axis=-1).transpose(1, 0, 2).astype(BF16)
    cwu = conv_w[:, :dff].reshape(-1, nch, cw)
    cwg = conv_w[:, dff:].reshape(-1, nch, cw)
    cwr = jnp.concatenate([cwu, cwg], axis=-1).transpose(1, 0, 2)
    cbr = jnp.concatenate([conv_b[:dff].reshape(nch, 1, cw), conv_b[dff:].reshape(nch, 1, cw)], axis=-1)
    wo = w_out.reshape(nch, cw, d).astype(BF16)
    kern = functools.partial(_ffn_kernel, nch=nch, cw=cw)
    return pl.pallas_call(
        kern,
        out_shape=jax.ShapeDtypeStruct((b, s, d), F32),
        grid=(b, s // ts),
        in_specs=[
            pl.BlockSpec((1, ts, d), lambda i, j: (i, j, 0)),
            _const_spec((1, d)),
            _const_spec((nch, d, 2 * cw)),
            _const_spec((nch, conv_w.shape[0], 2 * cw)),
            _const_spec((nch, 1, 2 * cw)),
            _const_spec((nch, cw, d)),
        ],
        out_specs=pl.BlockSpec((1, ts, d), lambda i, j: (i, j, 0)),
        scratch_shapes=[
            pltpu.VMEM((ts, d), BF16),
            pltpu.VMEM((nch, V7X_SUBLANES, 2 * cw), F32),
            pltpu.VMEM((V7X_SUBLANES + ts, 2 * cw), F32),
            pltpu.VMEM((V7X_SUBLANES + ts, 2 * cw), F32),
        ],
        compiler_params=_params(("arbitrary", "arbitrary")),
        name="conv_ffn",
    )(x, g.reshape(1, d), wug, cwr, cbr, wo)


def _rglru_kernel(x_ref, g_ref, win_ref, cw_ref, cb_ref, wax_ref, ba_ref, bx_ref, lam_ref, wo_ref,
                  o_ref, xcarry_scr, hcarry_scr, xr_scr, *, width, nblk):
    ts = x_ref.shape[1]
    x = x_ref[0]
    h = _rms(x, g_ref[...]).astype(BF16)

    @pl.when(pl.program_id(1) == 0)
    def _():
        xcarry_scr[...] = jnp.zeros_like(xcarry_scr)
        hcarry_scr[...] = jnp.zeros_like(hcarry_scr)

    z = jnp.dot(h, win_ref[...], preferred_element_type=F32)
    gate = z[:, :width]
    hl = V7X_SUBLANES
    xr_scr[0:hl, :] = xcarry_scr[...]
    xr_scr[hl:, :] = z[:, width:]
    xcarry_scr[...] = xr_scr[ts:ts + hl, :]
    kw = cw_ref.shape[0]
    xc = cb_ref[...]
    for k in range(kw):
        off = hl - (kw - 1) + k
        xc = xc + xr_scr[off:off + ts, :] * cw_ref[k:k + 1, :]

    xcb = xc.astype(BF16)
    ra, rx = [], []
    for n in range(nblk):
        ax = jnp.dot(xcb[:, n * RNN_BLOCK:(n + 1) * RNN_BLOCK], wax_ref[n], preferred_element_type=F32)
        ra.append(ax[:, :RNN_BLOCK])
        rx.append(ax[:, RNN_BLOCK:])
    r = jax.nn.sigmoid(jnp.concatenate(ra, axis=-1) + ba_ref[...])
    ig = jax.nn.sigmoid(jnp.concatenate(rx, axis=-1) + bx_ref[...])
    log_a = (-RG_C * jax.nn.softplus(-lam_ref[...])) * r
    a = jnp.exp(log_a)
    om = 1.0 - a * a
    bb = jnp.where(om > 0.0, om * lax.rsqrt(om), 0.0) * (ig * xc)

    ng = ts // hl
    a3 = a.reshape(ng, hl, width)
    b3 = bb.reshape(ng, hl, width)
    sub = lax.broadcasted_iota(jnp.int32, (1, hl, 1), 1)
    k = 1
    while k < hl:
        m = sub >= k
        b3 = jnp.where(m, a3 * pltpu.roll(b3, k, 1) + b3, b3)
        a3 = jnp.where(m, a3 * pltpu.roll(a3, k, 1), a3)
        k *= 2
    hprev = hcarry_scr[...]
    groups = []
    for gi in range(ng):
        hg = b3[gi] + a3[gi] * hprev
        groups.append(hg)
        hprev = hg[hl - 1:hl, :]
    hs = jnp.concatenate(groups, axis=0)
    hcarry_scr[...] = hprev

    y = (jax.nn.gelu(gate) * hs).astype(BF16)
    o_ref[0] = x + jnp.dot(y, wo_ref[...], preferred_element_type=F32)


def _rglru(x, g, w_in, conv_w, conv_b, w_a, b_a, w_x, b_x, lam, w_out):
    b, s, d = x.shape
    width = w_out.shape[0]
    nblk = w_a.shape[0]
    ts = SEQ_TILE
    wax = jnp.concatenate([w_a, w_x], axis=-1).astype(BF16)
    kern = functools.partial(_rglru_kernel, width=width, nblk=nblk)
    return pl.pallas_call(
        kern,
        out_shape=jax.ShapeDtypeStruct((b, s, d), F32),
        grid=(b, s // ts),
        in_specs=[
            pl.BlockSpec((1, ts, d), lambda i, j: (i, j, 0)),
            _const_spec((1, d)),
            _const_spec((d, 2 * width)),
            _const_spec(conv_w.shape),
            _const_spec((1, width)),
            _const_spec(wax.shape),
            _const_spec((1, width)),
            _const_spec((1, width)),
            _const_spec((1, width)),
            _const_spec((width, d)),
        ],
        out_specs=pl.BlockSpec((1, ts, d), lambda i, j: (i, j, 0)),
        scratch_shapes=[
            pltpu.VMEM((V7X_SUBLANES, width), F32),
            pltpu.VMEM((1, width), F32),
            pltpu.VMEM((V7X_SUBLANES + ts, width), F32),
        ],
        compiler_params=_params(("arbitrary", "arbitrary")),
        name="rglru_mixer",
    )(x, g.reshape(1, d), w_in.astype(BF16), conv_w, conv_b.reshape(1, width), wax,
      b_a.reshape(1, width), b_x.reshape(1, width), lam.reshape(1, width), w_out.astype(BF16))


def _pa_in_kernel(x_ref, g_ref, wu_ref, wk_ref, wqt_ref, wvt_ref, pw_ref, ps_ref, qg_ref, kg_ref, e_ref,
                  p_ref, qt_ref, k_ref, vt_ref, km_ref, ucarry_scr):
    ts = x_ref.shape[1]
    nhp = qt_ref.shape[1]
    nbt = ts // MOBA_BLOCK
    s = pl.program_id(1)
    h = _rms(x_ref[0], g_ref[...]).astype(BF16)

    @pl.when(s == 0)
    def _():
        ucarry_scr[...] = jnp.zeros_like(ucarry_scr)

    u = jnp.dot(h, wu_ref[...], preferred_element_type=F32)
    halo = ucarry_scr.shape[0]
    ext = jnp.concatenate([ucarry_scr[...], u], axis=0)
    ucarry_scr[...] = u[ts - halo:, :]
    pos = s * ts + lax.broadcasted_iota(jnp.int32, (ts, 1), 0)
    for gi, w in enumerate(POOL_WINDOWS):
        lanes = slice(gi * POOL_GROUP, (gi + 1) * POOL_GROUP)
        acc = ext[:, lanes]
        sh = 1
        while sh < w:
            acc = acc + pltpu.roll(acc, sh, 0)
            sh *= 2
        cnt = jnp.minimum(pos + 1, w).astype(F32)
        pg = acc[halo:] / cnt - u[:, lanes]
        out = jnp.dot(pg.astype(BF16), pw_ref[gi], preferred_element_type=F32) * ps_ref[:, lanes]
        p_ref[0, :, lanes] = out.astype(BF16)

    nt = (((1,), (1,)), ((), ()))
    nh = wqt_ref.shape[0] // HEAD_DIM
    qt = lax.dot_general(wqt_ref[...], h, nt, preferred_element_type=F32)
    q3 = qt.reshape(nh, HEAD_DIM, ts)
    ms = jnp.sum(q3 * q3, axis=1, keepdims=True) * (1.0 / HEAD_DIM)
    qn = ((q3 * lax.rsqrt(ms + EPS)).reshape(nh * HEAD_DIM, ts) * qg_ref[...]).astype(BF16)
    vt = lax.dot_general(wvt_ref[...], h, nt, preferred_element_type=F32).astype(BF16)

    kk = jnp.dot(h, wk_ref[...], preferred_element_type=F32)
    ssq = jnp.dot((kk * kk).astype(BF16), e_ref[...], preferred_element_type=F32)
    kn = kk * lax.rsqrt(ssq * (1.0 / HEAD_DIM) + EPS) * kg_ref[...]
    knb = kn.astype(BF16)

    hpw = 2 * HEAD_DIM
    for jb in range(nbt):
        cols = slice(jb * MOBA_BLOCK, (jb + 1) * MOBA_BLOCK)
        kmean = jnp.sum(kn[cols, :], axis=0, keepdims=True) * (1.0 / MOBA_BLOCK)
        for hp in range(nhp):
            rows = slice(hp * hpw, (hp + 1) * hpw)
            qt_ref[0, hp, jb] = qn[rows, cols]
            vt_ref[0, hp, jb] = vt[rows, cols]
            k_ref[0, hp, jb] = knb[cols, rows]
            km_ref[0, hp, pl.ds(s * nbt + jb, 1), :] = kmean[:, rows]


def _pa_in(x, g, w_in, pool_w, pool_scale, q_gain, k_gain):
    b, s, d = x.shape
    pw = pool_w.shape[0] * POOL_GROUP
    aw = (w_in.shape[1] - pw) // 3
    nh = aw // HEAD_DIM
    nhp = nh // 2
    nb = s // MOBA_BLOCK
    ts = SEQ_TILE
    hpw = 2 * HEAD_DIM
    wu = w_in[:, :pw].astype(BF16)
    wqt = w_in[:, pw:pw + aw].T.astype(BF16)
    wk = w_in[:, pw + aw:pw + 2 * aw].astype(BF16)
    wvt = w_in[:, pw + 2 * aw:].T.astype(BF16)
    qg = (jnp.tile(q_gain, nh) * (HEAD_DIM ** -0.5)).reshape(aw, 1)
    kg = jnp.tile(k_gain, nh).reshape(1, aw)
    hid = jnp.arange(aw) // HEAD_DIM
    e = (hid[:, None] == hid[None, :]).astype(BF16)
    return pl.pallas_call(
        _pa_in_kernel,
        out_shape=(
            jax.ShapeDtypeStruct((b, s, pw), BF16),
            jax.ShapeDtypeStruct((b, nhp, nb, hpw, MOBA_BLOCK), BF16),
            jax.ShapeDtypeStruct((b, nhp, nb, MOBA_BLOCK, hpw), BF16),
            jax.ShapeDtypeStruct((b, nhp, nb, hpw, MOBA_BLOCK), BF16),
            jax.ShapeDtypeStruct((b, nhp, nb, hpw), F32),
        ),
        grid=(b, s // ts),
        in_specs=[
            pl.BlockSpec((1, ts, d), lambda i, j: (i, j, 0)),
            _const_spec((1, d)),
            _const_spec((d, pw)),
            _const_spec((d, aw)),
            _const_spec((aw, d)),
            _const_spec((aw, d)),
            _const_spec(pool_w.shape),
            _const_spec((1, pw)),
            _const_spec((aw, 1)),
            _const_spec((1, aw)),
            _const_spec((aw, aw)),
        ],
        out_specs=(
            pl.BlockSpec((1, ts, pw), lambda i, j: (i, j, 0)),
            pl.BlockSpec((1, nhp, ts // MOBA_BLOCK, hpw, MOBA_BLOCK), lambda i, j: (i, 0, j, 0, 0)),
            pl.BlockSpec((1, nhp, ts // MOBA_BLOCK, MOBA_BLOCK, hpw), lambda i, j: (i, 0, j, 0, 0)),
            pl.BlockSpec((1, nhp, ts // MOBA_BLOCK, hpw, MOBA_BLOCK), lambda i, j: (i, 0, j, 0, 0)),
            pl.BlockSpec((1, nhp, nb, hpw), lambda i, j: (i, 0, 0, 0)),
        ),
        scratch_shapes=[pltpu.VMEM((max(POOL_WINDOWS), pw), F32)],
        compiler_params=_params(("arbitrary", "arbitrary")),
        name="pool_attn_in",
    )(x, g.reshape(1, d), wu, wk, wqt, wvt, pool_w.astype(BF16), pool_scale.reshape(1, pw), qg, kg, e)


def _moba_kernel(qt_ref, k_ref, vt_ref, km_ref, bias_ref, jsl_ref, o_ref, pv_scr, m_scr, w_scr):
    nqb = qt_ref.shape[2]
    nb = k_ref.shape[2]
    hpw, bq = qt_ref.shape[3], qt_ref.shape[4]
    iq = [pl.program_id(2) * nqb + qi for qi in range(nqb)]
    hrow = lax.broadcasted_iota(jnp.int32, (hpw, bq), 0) < HEAD_DIM
    mine = (hrow, jnp.logical_not(hrow))
    jj = lax.broadcasted_iota(jnp.int32, (nb, bq), 0)
    kmb = km_ref[0, 0].astype(BF16)
    zeros = jnp.zeros((hpw, bq), BF16)
    qh = [[jnp.where(mine[hh], qt_ref[0, 0, qi], zeros) for hh in range(2)] for qi in range(nqb)]
    m_scr[...] = jnp.zeros_like(m_scr)

    for qi in range(nqb):
        for hh in range(2):
            gate = jnp.dot(kmb, qh[qi][hh], preferred_element_type=F32)
            gt = jnp.where(jj < iq[qi], gate, -jnp.inf)
            rank = jnp.zeros((nb, bq), jnp.int32)
            for m in range(nb):
                gm = gt[m:m + 1, :]
                ahead = (gm > gt) | ((gm == gt) & (jj > m))
                rank = rank + ahead.astype(jnp.int32)
            keep = ((jj < iq[qi]) & (rank < MOBA_TOPK)) | (jj == iq[qi])
            w_scr[qi, hh] = jnp.where(keep, 0.0, NEG_INF) + jsl_ref[hh]

    def block(j):
        kj = k_ref[0, 0, j]
        vj = vt_ref[0, 0, j]
        for hh in range(2):
            vh = jnp.where(mine[hh], vj, jnp.ones_like(vj))
            for qi in range(nqb):
                kind = (j == iq[qi]).astype(jnp.int32)
                for qs in range(0, bq, QUERY_SPLIT):
                    ql = slice(qs, qs + QUERY_SPLIT)
                    st = jnp.dot(kj, qh[qi][hh][:, ql], preferred_element_type=F32) + bias_ref[kind, hh, :, ql]
                    mx = jnp.max(st, axis=0, keepdims=True)
                    p = jnp.exp(st - mx).astype(BF16)
                    pv_scr[j, qi, hh, :, ql] = jnp.dot(vh, p, preferred_element_type=F32)
                    m_scr[qi, hh, qs // QUERY_SPLIT, pl.ds(j, 1), :] = mx

    def group(t, carry):
        for u in range(KEY_BLOCKS_PER_STEP):
            block(jnp.minimum(KEY_BLOCKS_PER_STEP * t + u, nb - 1))
        return carry

    lax.fori_loop(0, iq[nqb - 1] // KEY_BLOCKS_PER_STEP + 1, group, 0)

    def comb(j, acc, qi, hh):
        return acc + pv_scr[j, qi, hh] * w_scr[qi, hh, pl.ds(j, 1), :]

    for qi in range(nqb):
        accs = []
        for hh in range(2):
            mr = jnp.concatenate([m_scr[qi, hh, t] for t in range(bq // QUERY_SPLIT)], axis=-1) + w_scr[qi, hh]
            w_scr[qi, hh] = jnp.exp(mr - jnp.max(mr, axis=0, keepdims=True))
            accs.append(lax.fori_loop(0, iq[qi] + 1, functools.partial(comb, qi=qi, hh=hh),
                                      jnp.zeros((hpw, bq), F32)))
        ot = jnp.concatenate([accs[0][:HEAD_DIM] / accs[0][HEAD_DIM:],
                              accs[1][HEAD_DIM:] / accs[1][:HEAD_DIM]], axis=0)
        o_ref[0, qi * bq:(qi + 1) * bq, :] = ot.T.astype(BF16)


def _moba(qt, k, vt, km):
    b, nhp, nb, hpw, bq = qt.shape
    nh = 2 * nhp
    nqb = Q_BLOCKS
    slopes = 2.0 ** (-8.0 * jnp.arange(1, nh + 1, dtype=F32) / nh)
    r = jnp.arange(MOBA_BLOCK, dtype=F32)
    past = slopes[:, None, None] * jnp.broadcast_to(r[:, None], (MOBA_BLOCK, bq))[None]
    causal = jnp.arange(MOBA_BLOCK)[:, None] <= jnp.arange(bq)[None, :]
    bias = jnp.stack([past, past + jnp.where(causal, 0.0, NEG_INF)[None]])
    jsl = slopes[:, None, None] * jnp.broadcast_to(
        (jnp.arange(nb, dtype=F32) * MOBA_BLOCK)[:, None], (nb, bq))[None]
    return pl.pallas_call(
        _moba_kernel,
        out_shape=jax.ShapeDtypeStruct((b, nb * bq, nhp * hpw), BF16),
        grid=(b, nhp, nb // nqb),
        in_specs=[
            pl.BlockSpec((1, 1, nqb, hpw, bq), lambda bi, hp, i: (bi, hp, i, 0, 0)),
            pl.BlockSpec((1, 1, nb, MOBA_BLOCK, hpw), lambda bi, hp, i: (bi, hp, 0, 0, 0)),
            pl.BlockSpec((1, 1, nb, hpw, MOBA_BLOCK), lambda bi, hp, i: (bi, hp, 0, 0, 0)),
            pl.BlockSpec((1, 1, nb, hpw), lambda bi, hp, i: (bi, hp, 0, 0)),
            pl.BlockSpec((2, 2, MOBA_BLOCK, bq), lambda bi, hp, i: (0, hp, 0, 0)),
            pl.BlockSpec((2, nb, bq), lambda bi, hp, i: (hp, 0, 0)),
        ],
        out_specs=pl.BlockSpec((1, nqb * bq, hpw), lambda bi, hp, i: (bi, i, hp)),
        scratch_shapes=[
            pltpu.VMEM((nb, nqb, 2, hpw, bq), F32),
            pltpu.VMEM((nqb, 2, bq // QUERY_SPLIT, nb, QUERY_SPLIT), F32),
            pltpu.VMEM((nqb, 2, nb, bq), F32),
        ],
        compiler_params=_params(("arbitrary", "arbitrary", "arbitrary")),
        name="moba_attention",
    )(qt, k, vt, km, bias, jsl)


def _pa_out_kernel(x_ref, p_ref, a_ref, wp_ref, wa_ref, o_ref):
    y = jnp.dot(p_ref[0], wp_ref[...], preferred_element_type=F32)
    y = y + jnp.dot(a_ref[0], wa_ref[...], preferred_element_type=F32)
    o_ref[0] = x_ref[0] + y


def _pa_out(x, p, a, w_out):
    b, s, d = x.shape
    pw, aw = p.shape[-1], a.shape[-1]
    ts = SEQ_TILE
    return pl.pallas_call(
        _pa_out_kernel,
        out_shape=jax.ShapeDtypeStruct((b, s, d), F32),
        grid=(b, s // ts),
        in_specs=[
            pl.BlockSpec((1, ts, d), lambda i, j: (i, j, 0)),
            pl.BlockSpec((1, ts, pw), lambda i, j: (i, j, 0)),
            pl.BlockSpec((1, ts, aw), lambda i, j: (i, j, 0)),
            _const_spec((pw, d)),
            _const_spec((aw, d)),
        ],
        out_specs=pl.BlockSpec((1, ts, d), lambda i, j: (i, j, 0)),
        compiler_params=_params(("arbitrary", "arbitrary")),
        name="pool_attn_out",
    )(x, p, a, w_out[:pw].astype(BF16), w_out[pw:].astype(BF16))


@jax.jit
def kernel(x, norm_mix, norm_ffn, pa_w_in, pa_pool_w, pa_pool_scale, pa_q_gain, pa_k_gain, pa_w_out,
           rg_w_in, rg_conv_w, rg_conv_b, rg_w_a, rg_b_a, rg_w_x, rg_b_x, rg_lambda, rg_w_out,
           ffn_w_in, ffn_conv_w, ffn_conv_b, ffn_w_out):
    depth = norm_mix.shape[0]
    for l in range(depth):
        j = l // 2
        if l % 2 == 0:
            p, qt, k, vt, km = _pa_in(x, norm_mix[l], pa_w_in[j], pa_pool_w[j], pa_pool_scale[j],
                                      pa_q_gain[j], pa_k_gain[j])
            a = _moba(qt, k, vt, km)
            x = _pa_out(x, p, a, pa_w_out[j])
        else:
            x = _rglru(x, norm_mix[l], rg_w_in[j], rg_conv_w[j], rg_conv_b[j], rg_w_a[j], rg_b_a[j],
                       rg_w_x[j], rg_b_x[j], rg_lambda[j], rg_w_out[j])
        x = _ffn(x, norm_ffn[l], ffn_w_in[l], ffn_conv_w[l], ffn_conv_b[l], ffn_w_out[l])
    return x
```

```python
import functools
import math

import jax
import jax.numpy as jnp
from jax import lax
from jax.experimental import pallas as pl
from jax.experimental.pallas import tpu as pltpu

F32 = jnp.float32
BF16 = jnp.bfloat16

EPS = 1e-6
NEG_INF = -1e30
POOL_WINDOWS = (2, 4, 8, 16)
POOL_GROUP = 128
HEAD_DIM = 64
MOBA_BLOCK = 256
MOBA_TOPK = 3
RG_C = 8.0
RNN_BLOCK = 128

V7X_LANES = 128
V7X_SUBLANES = 8
V7X_VMEM_LIMIT_BYTES = 56 * 1024 * 1024

SEQ_TILE = 512
FFN_CHUNK = 256
QUERY_SPLIT = 128
Q_BLOCKS = 4
KEY_BLOCKS_PER_STEP = 4


def _rms(x, g):
    ms = jnp.mean(x * x, axis=-1, keepdims=True)
    return x * lax.rsqrt(ms + EPS) * g


_GELU_K1 = -2.0 * math.sqrt(2.0 / math.pi) * math.log2(math.e)
_GELU_K2 = _GELU_K1 * 0.044715


def _gelu(x):
    return x / (1.0 + jnp.exp2((x * x * _GELU_K2 + _GELU_K1) * x))


def _const_spec(shape):
    zeros = (0,) * len(shape)
    return pl.BlockSpec(shape, lambda *_: zeros, pipeline_mode=pl.Buffered(1))


def _params(sem):
    return pltpu.CompilerParams(dimension_semantics=sem, vmem_limit_bytes=V7X_VMEM_LIMIT_BYTES)


def _ffn_kernel(x_ref, g_ref, wug_ref, cw_ref, cb_ref, wo_ref, o_ref, h_scr, carry_scr, z0_scr, z1_scr,
                *, nch, cw):
    z_scr = (z0_scr, z1_scr)
    ts = x_ref.shape[1]
    kw = cw_ref.shape[1]
    hl = V7X_SUBLANES
    h_scr[...] = _rms(x_ref[0], g_ref[...]).astype(BF16)
    o_ref[0] = x_ref[0]

    @pl.when(pl.program_id(1) == 0)
    def _():
        carry_scr[...] = jnp.zeros_like(carry_scr)

    def up(c, slot):
        z_scr[slot][hl:, :] = jnp.dot(h_scr[...], wug_ref[c], preferred_element_type=F32)

    def down(c, slot):
        zb = z_scr[slot]
        zb[0:hl, :] = carry_scr[c]
        carry_scr[c] = zb[ts:ts + hl, :]
        w = cw_ref[c]
        y = cb_ref[c] + zb[hl:, :] * w[kw - 1:kw]
        for k in range(kw - 1):
            off = hl - (kw - 1) + k
            y = y + zb[off:off + ts, :] * w[k:k + 1]
        a = (_gelu(y[:, cw:]) * y[:, :cw]).astype(BF16)
        o_ref[0] += jnp.dot(a, wo_ref[c], preferred_element_type=F32)

    up(0, 0)

    def pair(q, carry):
        c = 2 * q
        up(c + 1, 1)
        down(c, 0)
        up(c + 2, 0)
        down(c + 1, 1)
        return carry

    npairs = (nch - 1) // 2
    lax.fori_loop(0, npairs, pair, 0)
    if nch % 2 == 0:
        up(nch - 1, 1)
        down(nch - 2, 0)
        down(nch - 1, 1)
    else:
        down(nch - 1, 0)


def _ffn(x, g, w_in, conv_w, conv_b, w_out):
    b, s, d = x.shape
    dff = w_out.shape[0]
    cw = FFN_CHUNK
    nch = dff // cw
    ts = SEQ_TILE
    wu = w_in[:, :dff].reshape(d, nch, cw)
    wg = w_in[:, dff:].reshape(d, nch, cw)
    wug = jnp.concatenate([wu, wg], axis=-1).transpose(1, 0, 2).astype(BF16)
    cwu = conv_w[:, :dff].reshape(-1, nch, cw)
    cwg = conv_w[:, dff:].reshape(-1, nch, cw)
    cwr = jnp.concatenate([cwu, cwg], axis=-1).transpose(1, 0, 2)
    cbr = jnp.concatenate([conv_b[:dff].reshape(nch, 1, cw), conv_b[dff:].reshape(nch, 1, cw)], axis=-1)
    wo = w_out.reshape(nch, cw, d).astype(BF16)
    kern = functools.partial(_ffn_kernel, nch=nch, cw=cw)
    return pl.pallas_call(
        kern,
        out_shape=jax.ShapeDtypeStruct((b, s, d), F32),
        grid=(b, s // ts),
        in_specs=[
            pl.BlockSpec((1, ts, d), lambda i, j: (i, j, 0)),
            _const_spec((1, d)),
            _const_spec((nch, d, 2 * cw)),
            _const_spec((nch, conv_w.shape[0], 2 * cw)),
            _const_spec((nch, 1, 2 * cw)),
            _const_spec((nch, cw, d)),
        ],
        out_specs=pl.BlockSpec((1, ts, d), lambda i, j: (i, j, 0)),
        scratch_shapes=[
            pltpu.VMEM((ts, d), BF16),
            pltpu.VMEM((nch, V7X_SUBLANES, 2 * cw), F32),
            pltpu.VMEM((V7X_SUBLANES + ts, 2 * cw), F32),
            pltpu.VMEM((V7X_SUBLANES + ts, 2 * cw), F32),
        ],
        compiler_params=_params(("arbitrary", "arbitrary")),
        name="conv_ffn",
    )(x, g.reshape(1, d), wug, cwr, cbr, wo)


def _rglru_kernel(x_ref, g_ref, win_ref, cw_ref, cb_ref, wax_ref, ba_ref, bx_ref, lam_ref, wo_ref,
                  o_ref, xcarry_scr, hcarry_scr, xr_scr, *, width, nblk):
    ts = x_ref.shape[1]
    x = x_ref[0]
    h = _rms(x, g_ref[...]).astype(BF16)

    @pl.when(pl.program_id(1) == 0)
    def _():
        xcarry_scr[...] = jnp.zeros_like(xcarry_scr)
        hcarry_scr[...] = jnp.zeros_like(hcarry_scr)

    z = jnp.dot(h, win_ref[...], preferred_element_type=F32)
    gate = z[:, :width]
    hl = V7X_SUBLANES
    xr_scr[0:hl, :] = xcarry_scr[...]
    xr_scr[hl:, :] = z[:, width:]
    xcarry_scr[...] = xr_scr[ts:ts + hl, :]
    kw = cw_ref.shape[0]
    xc = cb_ref[...]
    for k in range(kw):
        off = hl - (kw - 1) + k
        xc = xc + xr_scr[off:off + ts, :] * cw_ref[k:k + 1, :]

    xcb = xc.astype(BF16)
    ra, rx = [], []
    for n in range(nblk):
        ax = jnp.dot(xcb[:, n * RNN_BLOCK:(n + 1) * RNN_BLOCK], wax_ref[n], preferred_element_type=F32)
        ra.append(ax[:, :RNN_BLOCK])
        rx.append(ax[:, RNN_BLOCK:])
    r = jax.nn.sigmoid(jnp.concatenate(ra, axis=-1) + ba_ref[...])
    ig = jax.nn.sigmoid(jnp.concatenate(rx, axis=-1) + bx_ref[...])
    log_a = (-RG_C * jax.nn.softplus(-lam_ref[...])) * r
    a = jnp.exp(log_a)
    om = 1.0 - a * a
    bb = jnp.where(om > 0.0, om * lax.rsqrt(om), 0.0) * (ig * xc)

    ng = ts // hl
    a3 = a.reshape(ng, hl, width)
    b3 = bb.reshape(ng, hl, width)
    sub = lax.broadcasted_iota(jnp.int32, (1, hl, 1), 1)
    k = 1
    while k < hl:
        m = sub >= k
        b3 = jnp.where(m, a3 * pltpu.roll(b3, k, 1) + b3, b3)
        a3 = jnp.where(m, a3 * pltpu.roll(a3, k, 1), a3)
        k *= 2
    hprev = hcarry_scr[...]
    groups = []
    for gi in range(ng):
        hg = b3[gi] + a3[gi] * hprev
        groups.append(hg)
        hprev = hg[hl - 1:hl, :]
    hs = jnp.concatenate(groups, axis=0)
    hcarry_scr[...] = hprev

    y = (_gelu(gate) * hs).astype(BF16)
    o_ref[0] = x + jnp.dot(y, wo_ref[...], preferred_element_type=F32)


def _rglru(x, g, w_in, conv_w, conv_b, w_a, b_a, w_x, b_x, lam, w_out):
    b, s, d = x.shape
    width = w_out.shape[0]
    nblk = w_a.shape[0]
    ts = SEQ_TILE
    wax = jnp.concatenate([w_a, w_x], axis=-1).astype(BF16)
    kern = functools.partial(_rglru_kernel, width=width, nblk=nblk)
    return pl.pallas_call(
        kern,
        out_shape=jax.ShapeDtypeStruct((b, s, d), F32),
        grid=(b, s // ts),
        in_specs=[
            pl.BlockSpec((1, ts, d), lambda i, j: (i, j, 0)),
            _const_spec((1, d)),
            _const_spec((d, 2 * width)),
            _const_spec(conv_w.shape),
            _const_spec((1, width)),
            _const_spec(wax.shape),
            _const_spec((1, width)),
            _const_spec((1, width)),
            _const_spec((1, width)),
            _const_spec((width, d)),
        ],
        out_specs=pl.BlockSpec((1, ts, d), lambda i, j: (i, j, 0)),
        scratch_shapes=[
            pltpu.VMEM((V7X_SUBLANES, width), F32),
            pltpu.VMEM((1, width), F32),
            pltpu.VMEM((V7X_SUBLANES + ts, width), F32),
        ],
        compiler_params=_params(("arbitrary", "arbitrary")),
        name="rglru_mixer",
    )(x, g.reshape(1, d), w_in.astype(BF16), conv_w, conv_b.reshape(1, width), wax,
      b_a.reshape(1, width), b_x.reshape(1, width), lam.reshape(1, width), w_out.astype(BF16))


def _pa_in_kernel(x_ref, g_ref, wu_ref, wk_ref, wqt_ref, wvt_ref, pw_ref, ps_ref, qg_ref, kg_ref, e_ref,
                  p_ref, qt_ref, k_ref, vt_ref, km_ref, ucarry_scr):
    ts = x_ref.shape[1]
    nhp = qt_ref.shape[1]
    nbt = ts // MOBA_BLOCK
    s = pl.program_id(1)
    h = _rms(x_ref[0], g_ref[...]).astype(BF16)

    @pl.when(s == 0)
    def _():
        ucarry_scr[...] = jnp.zeros_like(ucarry_scr)

    u = jnp.dot(h, wu_ref[...], preferred_element_type=F32)
    halo = ucarry_scr.shape[0]
    ext = jnp.concatenate([ucarry_scr[...], u], axis=0)
    ucarry_scr[...] = u[ts - halo:, :]
    pos = s * ts + lax.broadcasted_iota(jnp.int32, (ts, 1), 0)
    for gi, w in enumerate(POOL_WINDOWS):
        lanes = slice(gi * POOL_GROUP, (gi + 1) * POOL_GROUP)
        acc = ext[:, lanes]
        sh = 1
        while sh < w:
            acc = acc + pltpu.roll(acc, sh, 0)
            sh *= 2
        cnt = jnp.minimum(pos + 1, w).astype(F32)
        pg = acc[halo:] / cnt - u[:, lanes]
        out = jnp.dot(pg.astype(BF16), pw_ref[gi], preferred_element_type=F32) * ps_ref[:, lanes]
        p_ref[0, :, lanes] = out.astype(BF16)

    nt = (((1,), (1,)), ((), ()))
    nh = wqt_ref.shape[0] // HEAD_DIM
    qt = lax.dot_general(wqt_ref[...], h, nt, preferred_element_type=F32)
    q3 = qt.reshape(nh, HEAD_DIM, ts)
    ms = jnp.sum(q3 * q3, axis=1, keepdims=True) * (1.0 / HEAD_DIM)
    qn = ((q3 * lax.rsqrt(ms + EPS)).reshape(nh * HEAD_DIM, ts) * qg_ref[...]).astype(BF16)
    vt = lax.dot_general(wvt_ref[...], h, nt, preferred_element_type=F32).astype(BF16)

    kk = jnp.dot(h, wk_ref[...], preferred_element_type=F32)
    ssq = jnp.dot((kk * kk).astype(BF16), e_ref[...], preferred_element_type=F32)
    kn = kk * lax.rsqrt(ssq * (1.0 / HEAD_DIM) + EPS) * kg_ref[...]
    knb = kn.astype(BF16)

    hpw = 2 * HEAD_DIM
    for jb in range(nbt):
        cols = slice(jb * MOBA_BLOCK, (jb + 1) * MOBA_BLOCK)
        kmean = jnp.sum(kn[cols, :], axis=0, keepdims=True) * (1.0 / MOBA_BLOCK)
        for hp in range(nhp):
            rows = slice(hp * hpw, (hp + 1) * hpw)
            qt_ref[0, hp, jb] = qn[rows, cols]
            vt_ref[0, hp, jb] = vt[rows, cols]
            k_ref[0, hp, jb] = knb[cols, rows]
            km_ref[0, hp, pl.ds(s * nbt + jb, 1), :] = kmean[:, rows]


def _pa_in(x, g, w_in, pool_w, pool_scale, q_gain, k_gain):
    b, s, d = x.shape
    pw = pool_w.shape[0] * POOL_GROUP
    aw = (w_in.shape[1] - pw) // 3
    nh = aw // HEAD_DIM
    nhp = nh // 2
    nb = s // MOBA_BLOCK
    ts = SEQ_TILE
    hpw = 2 * HEAD_DIM
    wu = w_in[:, :pw].astype(BF16)
    wqt = w_in[:, pw:pw + aw].T.astype(BF16)
    wk = w_in[:, pw + aw:pw + 2 * aw].astype(BF16)
    wvt = w_in[:, pw + 2 * aw:].T.astype(BF16)
    qg = (jnp.tile(q_gain, nh) * (HEAD_DIM ** -0.5)).reshape(aw, 1)
    kg = jnp.tile(k_gain, nh).reshape(1, aw)
    hid = jnp.arange(aw) // HEAD_DIM
    e = (hid[:, None] == hid[None, :]).astype(BF16)
    return pl.pallas_call(
        _pa_in_kernel,
        out_shape=(
            jax.ShapeDtypeStruct((b, s, pw), BF16),
            jax.ShapeDtypeStruct((b, nhp, nb, hpw, MOBA_BLOCK), BF16),
            jax.ShapeDtypeStruct((b, nhp, nb, MOBA_BLOCK, hpw), BF16),
            jax.ShapeDtypeStruct((b, nhp, nb, hpw, MOBA_BLOCK), BF16),
            jax.ShapeDtypeStruct((b, nhp, nb, hpw), F32),
        ),
        grid=(b, s // ts),
        in_specs=[
            pl.BlockSpec((1, ts, d), lambda i, j: (i, j, 0)),
            _const_spec((1, d)),
            _const_spec((d, pw)),
            _const_spec((d, aw)),
            _const_spec((aw, d)),
            _const_spec((aw, d)),
            _const_spec(pool_w.shape),
            _const_spec((1, pw)),
            _const_spec((aw, 1)),
            _const_spec((1, aw)),
            _const_spec((aw, aw)),
        ],
        out_specs=(
            pl.BlockSpec((1, ts, pw), lambda i, j: (i, j, 0)),
            pl.BlockSpec((1, nhp, ts // MOBA_BLOCK, hpw, MOBA_BLOCK), lambda i, j: (i, 0, j, 0, 0)),
            pl.BlockSpec((1, nhp, ts // MOBA_BLOCK, MOBA_BLOCK, hpw), lambda i, j: (i, 0, j, 0, 0)),
            pl.BlockSpec((1, nhp, ts // MOBA_BLOCK, hpw, MOBA_BLOCK), lambda i, j: (i, 0, j, 0, 0)),
            pl.BlockSpec((1, nhp, nb, hpw), lambda i, j: (i, 0, 0, 0)),
        ),
        scratch_shapes=[pltpu.VMEM((max(POOL_WINDOWS), pw), F32)],
        compiler_params=_params(("arbitrary", "arbitrary")),
        name="pool_attn_in",
    )(x, g.reshape(1, d), wu, wk, wqt, wvt, pool_w.astype(BF16), pool_scale.reshape(1, pw), qg, kg, e)


def _moba_kernel(qt_ref, k_ref, vt_ref, km_ref, bias_ref, jsl_ref, o_ref, pv_scr, m_scr, w_scr):
    nqb = qt_ref.shape[2]
    nb = k_ref.shape[2]
    hpw, bq = qt_ref.shape[3], qt_ref.shape[4]
    iq = [pl.program_id(2) * nqb + qi for qi in range(nqb)]
    hrow = lax.broadcasted_iota(jnp.int32, (hpw, bq), 0) < HEAD_DIM
    mine = (hrow, jnp.logical_not(hrow))
    jj = lax.broadcasted_iota(jnp.int32, (nb, bq), 0)
    kmb = km_ref[0, 0].astype(BF16)
    zeros = jnp.zeros((hpw, bq), BF16)
    qh = [[jnp.where(mine[hh], qt_ref[0, 0, qi], zeros) for hh in range(2)] for qi in range(nqb)]
    m_scr[...] = jnp.zeros_like(m_scr)

    for qi in range(nqb):
        for hh in range(2):
            gate = jnp.dot(kmb, qh[qi][hh], preferred_element_type=F32)
            gt = jnp.where(jj < iq[qi], gate, -jnp.inf)
            rank = jnp.zeros((nb, bq), jnp.int32)
            for m in range(nb):
                gm = gt[m:m + 1, :]
                ahead = (gm > gt) | ((gm == gt) & (jj > m))
                rank = rank + ahead.astype(jnp.int32)
            keep = ((jj < iq[qi]) & (rank < MOBA_TOPK)) | (jj == iq[qi])
            w_scr[qi, hh] = jnp.where(keep, 0.0, NEG_INF) + jsl_ref[hh]

    def block(j):
        kj = k_ref[0, 0, j]
        vj = vt_ref[0, 0, j]
        for hh in range(2):
            vh = jnp.where(mine[hh], vj, jnp.ones_like(vj))
            for qi in range(nqb):
                kind = (j == iq[qi]).astype(jnp.int32)
                for qs in range(0, bq, QUERY_SPLIT):
                    ql = slice(qs, qs + QUERY_SPLIT)
                    st = jnp.dot(kj, qh[qi][hh][:, ql], preferred_element_type=F32) + bias_ref[kind, hh, :, ql]
                    mx = jnp.max(st, axis=0, keepdims=True)
                    p = jnp.exp(st - mx).astype(BF16)
                    pv_scr[j, qi, hh, :, ql] = jnp.dot(vh, p, preferred_element_type=F32)
                    m_scr[qi, hh, qs // QUERY_SPLIT, pl.ds(j, 1), :] = mx

    def group(t, carry):
        for u in range(KEY_BLOCKS_PER_STEP):
            block(jnp.minimum(KEY_BLOCKS_PER_STEP * t + u, nb - 1))
        return carry

    lax.fori_loop(0, iq[nqb - 1] // KEY_BLOCKS_PER_STEP + 1, group, 0)

    def comb(j, acc, qi, hh):
        return acc + pv_scr[j, qi, hh] * w_scr[qi, hh, pl.ds(j, 1), :]

    for qi in range(nqb):
        accs = []
        for hh in range(2):
            mr = jnp.concatenate([m_scr[qi, hh, t] for t in range(bq // QUERY_SPLIT)], axis=-1) + w_scr[qi, hh]
            w_scr[qi, hh] = jnp.exp(mr - jnp.max(mr, axis=0, keepdims=True))
            accs.append(lax.fori_loop(0, iq[qi] + 1, functools.partial(comb, qi=qi, hh=hh),
                                      jnp.zeros((hpw, bq), F32)))
        ot = jnp.concatenate([accs[0][:HEAD_DIM] / accs[0][HEAD_DIM:],
                              accs[1][HEAD_DIM:] / accs[1][:HEAD_DIM]], axis=0)
        o_ref[0, qi * bq:(qi + 1) * bq, :] = ot.T.astype(BF16)


def _moba(qt, k, vt, km):
    b, nhp, nb, hpw, bq = qt.shape
    nh = 2 * nhp
    nqb = Q_BLOCKS
    slopes = 2.0 ** (-8.0 * jnp.arange(1, nh + 1, dtype=F32) / nh)
    r = jnp.arange(MOBA_BLOCK, dtype=F32)
    past = slopes[:, None, None] * jnp.broadcast_to(r[:, None], (MOBA_BLOCK, bq))[None]
    causal = jnp.arange(MOBA_BLOCK)[:, None] <= jnp.arange(bq)[None, :]
    bias = jnp.stack([past, past + jnp.where(causal, 0.0, NEG_INF)[None]])
    jsl = slopes[:, None, None] * jnp.broadcast_to(
        (jnp.arange(nb, dtype=F32) * MOBA_BLOCK)[:, None], (nb, bq))[None]
    return pl.pallas_call(
        _moba_kernel,
        out_shape=jax.ShapeDtypeStruct((b, nb * bq, nhp * hpw), BF16),
        grid=(b, nhp, nb // nqb),
        in_specs=[
            pl.BlockSpec((1, 1, nqb, hpw, bq), lambda bi, hp, i: (bi, hp, i, 0, 0)),
            pl.BlockSpec((1, 1, nb, MOBA_BLOCK, hpw), lambda bi, hp, i: (bi, hp, 0, 0, 0)),
            pl.BlockSpec((1, 1, nb, hpw, MOBA_BLOCK), lambda bi, hp, i: (bi, hp, 0, 0, 0)),
            pl.BlockSpec((1, 1, nb, hpw), lambda bi, hp, i: (bi, hp, 0, 0)),
            pl.BlockSpec((2, 2, MOBA_BLOCK, bq), lambda bi, hp, i: (0, hp, 0, 0)),
            pl.BlockSpec((2, nb, bq), lambda bi, hp, i: (hp, 0, 0)),
        ],
        out_specs=pl.BlockSpec((1, nqb * bq, hpw), lambda bi, hp, i: (bi, i, hp)),
        scratch_shapes=[
            pltpu.VMEM((nb, nqb, 2, hpw, bq), F32),
            pltpu.VMEM((nqb, 2, bq // QUERY_SPLIT, nb, QUERY_SPLIT), F32),
            pltpu.VMEM((nqb, 2, nb, bq), F32),
        ],
        compiler_params=_params(("arbitrary", "arbitrary", "arbitrary")),
        name="moba_attention",
    )(qt, k, vt, km, bias, jsl)


def _pa_out_kernel(x_ref, p_ref, a_ref, wp_ref, wa_ref, o_ref):
    y = jnp.dot(p_ref[0], wp_ref[...], preferred_element_type=F32)
    y = y + jnp.dot(a_ref[0], wa_ref[...], preferred_element_type=F32)
    o_ref[0] = x_ref[0] + y


def _pa_out(x, p, a, w_out):
    b, s, d = x.shape
    pw, aw = p.shape[-1], a.shape[-1]
    ts = SEQ_TILE
    return pl.pallas_call(
        _pa_out_kernel,
        out_shape=jax.ShapeDtypeStruct((b, s, d), F32),
        grid=(b, s // ts),
        in_specs=[
            pl.BlockSpec((1, ts, d), lambda i, j: (i, j, 0)),
            pl.BlockSpec((1, ts, pw), lambda i, j: (i, j, 0)),
            pl.BlockSpec((1, ts, aw), lambda i, j: (i, j, 0)),
            _const_spec((pw, d)),
            _const_spec((aw, d)),
        ],
        out_specs=pl.BlockSpec((1, ts, d), lambda i, j: (i, j, 0)),
        compiler_params=_params(("arbitrary", "arbitrary")),
        name="pool_attn_out",
    )(x, p, a, w_out[:pw].astype(BF16), w_out[pw:].astype(BF16))


@jax.jit
def kernel(x, norm_mix, norm_ffn, pa_w_in, pa_pool_w, pa_pool_scale, pa_q_gain, pa_k_gain, pa_w_out,
           rg_w_in, rg_conv_w, rg_conv_b, rg_w_a, rg_b_a, rg_w_x, rg_b_x, rg_lambda, rg_w_out,
           ffn_w_in, ffn_conv_w, ffn_conv_b, ffn_w_out):
    depth = norm_mix.shape[0]
    for l in range(depth):
        j = l // 2
        if l % 2 == 0:
            p, qt, k, vt, km = _pa_in(x, norm_mix[l], pa_w_in[j], pa_pool_w[j], pa_pool_scale[j],
                                      pa_q_gain[j], pa_k_gain[j])
            a = _moba(qt, k, vt, km)
            x = _pa_out(x, p, a, pa_w_out[j])
        else:
            x = _rglru(x, norm_mix[l], rg_w_in[j], rg_conv_w[j], rg_conv_b[j], rg_w_a[j], rg_b_a[j],
                       rg_w_x[j], rg_b_x[j], rg_lambda[j], rg_w_out[j])
        x = _ffn(x, norm_ffn[l], ffn_w_in[l], ffn_conv_w[l], ffn_conv_b[l], ffn_w_out[l])
    return x
```

```python
import functools
import math

import jax
import jax.numpy as jnp
from jax import lax
from jax.experimental import pallas as pl
from jax.experimental.pallas import tpu as pltpu

F32 = jnp.float32
BF16 = jnp.bfloat16

EPS = 1e-6
NEG_INF = -1e30
POOL_WINDOWS = (2, 4, 8, 16)
POOL_GROUP = 128
HEAD_DIM = 64
MOBA_BLOCK = 256
MOBA_TOPK = 3
RG_C = 8.0
RNN_BLOCK = 128

V7X_LANES = 128
V7X_SUBLANES = 8
V7X_VMEM_LIMIT_BYTES = 56 * 1024 * 1024

SEQ_TILE = 512
FFN_CHUNK = 256
QUERY_SPLIT = 128
Q_BLOCKS = 4


def _rms(x, g):
    ms = jnp.mean(x * x, axis=-1, keepdims=True)
    return x * lax.rsqrt(ms + EPS) * g


_GELU_K1 = -2.0 * math.sqrt(2.0 / math.pi) * math.log2(math.e)
_GELU_K2 = _GELU_K1 * 0.044715


def _gelu(x):
    return x / (1.0 + jnp.exp2((x * x * _GELU_K2 + _GELU_K1) * x))


def _const_spec(shape):
    zeros = (0,) * len(shape)
    return pl.BlockSpec(shape, lambda *_: zeros, pipeline_mode=pl.Buffered(1))


def _params(sem):
    return pltpu.CompilerParams(dimension_semantics=sem, vmem_limit_bytes=V7X_VMEM_LIMIT_BYTES)


def _segment_rows(gi, seg):
    s, p = divmod(V7X_SUBLANES * gi, seg)
    return pl.ds(p * V7X_SUBLANES + s, V7X_SUBLANES, stride=V7X_SUBLANES)


def _conv_segment_major(zp, prev_tail, w, bias):
    kw, hl, ts = w.shape[0], V7X_SUBLANES, zp.shape[0]
    sub = lax.broadcasted_iota(jnp.int32, (hl, 1), 0)
    tail = zp[ts - (kw - 1) * hl:]
    halos = []
    for i in range(kw - 1):
        grp = slice(i * hl, (i + 1) * hl)
        halos.append(pltpu.roll(jnp.where(sub == hl - 1, prev_tail[grp], tail[grp]), 1, 0))
    ext = jnp.concatenate(halos + [zp], axis=0)
    y = bias
    for k in range(kw):
        y = y + ext[k * hl:k * hl + ts] * w[k:k + 1]
    return y, tail


def _ffn_kernel(x_ref, g_ref, wug_ref, cw_ref, cb_ref, wo_ref, o_ref, h_scr, carry_scr, z0_scr, z1_scr,
                *, nch, cw):
    z_scr = (z0_scr, z1_scr)
    ts = x_ref.shape[1]
    kw = cw_ref.shape[1]
    hl = V7X_SUBLANES
    h_scr[...] = _rms(x_ref[0], g_ref[...]).astype(BF16)
    o_ref[0] = x_ref[0]

    @pl.when(pl.program_id(1) == 0)
    def _():
        carry_scr[...] = jnp.zeros_like(carry_scr)

    def up(c, slot):
        z_scr[slot][hl:, :] = jnp.dot(h_scr[...], wug_ref[c], preferred_element_type=F32)

    def down(c, slot):
        zb = z_scr[slot]
        zb[0:hl, :] = carry_scr[c]
        carry_scr[c] = zb[ts:ts + hl, :]
        w = cw_ref[c]
        y = cb_ref[c] + zb[hl:, :] * w[kw - 1:kw]
        for k in range(kw - 1):
            off = hl - (kw - 1) + k
            y = y + zb[off:off + ts, :] * w[k:k + 1]
        a = (_gelu(y[:, cw:]) * y[:, :cw]).astype(BF16)
        o_ref[0] += jnp.dot(a, wo_ref[c], preferred_element_type=F32)

    up(0, 0)

    def pair(q, carry):
        c = 2 * q
        up(c + 1, 1)
        down(c, 0)
        up(c + 2, 0)
        down(c + 1, 1)
        return carry

    npairs = (nch - 1) // 2
    lax.fori_loop(0, npairs, pair, 0)
    if nch % 2 == 0:
        up(nch - 1, 1)
        down(nch - 2, 0)
        down(nch - 1, 1)
    else:
        down(nch - 1, 0)


def _ffn(x, g, w_in, conv_w, conv_b, w_out):
    b, s, d = x.shape
    dff = w_out.shape[0]
    cw = FFN_CHUNK
    nch = dff // cw
    ts = SEQ_TILE
    wu = w_in[:, :dff].reshape(d, nch, cw)
    wg = w_in[:, dff:].reshape(d, nch, cw)
    wug = jnp.concatenate([wu, wg], axis=-1).transpose(1, 0, 2).astype(BF16)
    cwu = conv_w[:, :dff].reshape(-1, nch, cw)
    cwg = conv_w[:, dff:].reshape(-1, nch, cw)
    cwr = jnp.concatenate([cwu, cwg], axis=-1).transpose(1, 0, 2)
    cbr = jnp.concatenate([conv_b[:dff].reshape(nch, 1, cw), conv_b[dff:].reshape(nch, 1, cw)], axis=-1)
    wo = w_out.reshape(nch, cw, d).astype(BF16)
    kern = functools.partial(_ffn_kernel, nch=nch, cw=cw)
    return pl.pallas_call(
        kern,
        out_shape=jax.ShapeDtypeStruct((b, s, d), F32),
        grid=(b, s // ts),
        in_specs=[
            pl.BlockSpec((1, ts, d), lambda i, j: (i, j, 0)),
            _const_spec((1, d)),
            _const_spec((nch, d, 2 * cw)),
            _const_spec((nch, conv_w.shape[0], 2 * cw)),
            _const_spec((nch, 1, 2 * cw)),
            _const_spec((nch, cw, d)),
        ],
        out_specs=pl.BlockSpec((1, ts, d), lambda i, j: (i, j, 0)),
        scratch_shapes=[
            pltpu.VMEM((ts, d), BF16),
            pltpu.VMEM((nch, V7X_SUBLANES, 2 * cw), F32),
            pltpu.VMEM((V7X_SUBLANES + ts, 2 * cw), F32),
            pltpu.VMEM((V7X_SUBLANES + ts, 2 * cw), F32),
        ],
        compiler_params=_params(("arbitrary", "arbitrary")),
        name="conv_ffn",
    )(x, g.reshape(1, d), wug, cwr, cbr, wo)


def _rglru_kernel(x_ref, g_ref, win_ref, cw_ref, cb_ref, wax_ref, ba_ref, bx_ref, lam_ref, wo_ref,
                  o_ref, xcarry_scr, hcarry_scr, *slabs, width, nblk):
    ts, d = x_ref.shape[1], x_ref.shape[2]
    hl = V7X_SUBLANES
    seg = ts // hl
    nw = width // V7X_LANES
    zs, osl = slabs[:2 * nw], slabs[2 * nw:]
    x = x_ref[0]
    h = _rms(x, g_ref[...]).astype(BF16)

    @pl.when(pl.program_id(1) == 0)
    def _():
        xcarry_scr[...] = jnp.zeros_like(xcarry_scr)
        hcarry_scr[...] = jnp.zeros_like(hcarry_scr)

    z = jnp.dot(h, win_ref[...], preferred_element_type=F32)
    for gi in range(seg):
        t0 = hl * gi
        for l in range(2 * nw):
            zs[l][_segment_rows(gi, seg), :] = z[t0:t0 + hl, l * V7X_LANES:(l + 1) * V7X_LANES]
    gate = jnp.concatenate([zs[l][...] for l in range(nw)], axis=-1)
    xr = jnp.concatenate([zs[nw + l][...] for l in range(nw)], axis=-1)
    xc, tail = _conv_segment_major(xr, xcarry_scr[...], cw_ref[...], cb_ref[...])
    xcarry_scr[...] = tail

    xcb = xc.astype(BF16)
    ra, rx = [], []
    for n in range(nblk):
        ax = jnp.dot(xcb[:, n * RNN_BLOCK:(n + 1) * RNN_BLOCK], wax_ref[n], preferred_element_type=F32)
        ra.append(ax[:, :RNN_BLOCK])
        rx.append(ax[:, RNN_BLOCK:])
    r = jax.nn.sigmoid(jnp.concatenate(ra, axis=-1) + ba_ref[...])
    ig = jax.nn.sigmoid(jnp.concatenate(rx, axis=-1) + bx_ref[...])
    log_a = (-RG_C * jax.nn.softplus(-lam_ref[...])) * r
    a = jnp.exp(log_a)
    om = 1.0 - a * a
    bb = jnp.where(om > 0.0, om * lax.rsqrt(om), 0.0) * (ig * xc)

    a3 = a.reshape(seg, hl, width)
    b3 = bb.reshape(seg, hl, width)
    hloc, ploc = [b3[0]], [a3[0]]
    for p in range(1, seg):
        hloc.append(a3[p] * hloc[-1] + b3[p])
        ploc.append(a3[p] * ploc[-1])
    sub = lax.broadcasted_iota(jnp.int32, (hl, 1), 0)
    base = pltpu.roll(hcarry_scr[...], 1, 0)
    enter = base
    for _ in range(hl - 1):
        enter = jnp.where(sub == 0, base, pltpu.roll(hloc[-1] + ploc[-1] * enter, 1, 0))
    hcarry_scr[...] = hloc[-1] + ploc[-1] * enter
    hs = jnp.concatenate([hloc[p] + ploc[p] * enter for p in range(seg)], axis=0)

    y = (_gelu(gate) * hs).astype(BF16)
    out = jnp.dot(y, wo_ref[...], preferred_element_type=F32)
    for l in range(d // V7X_LANES):
        osl[l][...] = out[:, l * V7X_LANES:(l + 1) * V7X_LANES]
    for gi in range(seg):
        t0 = hl * gi
        rows = [osl[l][_segment_rows(gi, seg), :] for l in range(d // V7X_LANES)]
        o_ref[0, t0:t0 + hl, :] = x[t0:t0 + hl] + jnp.concatenate(rows, axis=-1)


def _rglru(x, g, w_in, conv_w, conv_b, w_a, b_a, w_x, b_x, lam, w_out):
    b, s, d = x.shape
    width = w_out.shape[0]
    nblk = w_a.shape[0]
    ts = SEQ_TILE
    wax = jnp.concatenate([w_a, w_x], axis=-1).astype(BF16)
    kern = functools.partial(_rglru_kernel, width=width, nblk=nblk)
    return pl.pallas_call(
        kern,
        out_shape=jax.ShapeDtypeStruct((b, s, d), F32),
        grid=(b, s // ts),
        in_specs=[
            pl.BlockSpec((1, ts, d), lambda i, j: (i, j, 0)),
            _const_spec((1, d)),
            _const_spec((d, 2 * width)),
            _const_spec(conv_w.shape),
            _const_spec((1, width)),
            _const_spec(wax.shape),
            _const_spec((1, width)),
            _const_spec((1, width)),
            _const_spec((1, width)),
            _const_spec((width, d)),
        ],
        out_specs=pl.BlockSpec((1, ts, d), lambda i, j: (i, j, 0)),
        scratch_shapes=[
            pltpu.VMEM(((conv_w.shape[0] - 1) * V7X_SUBLANES, width), F32),
            pltpu.VMEM((V7X_SUBLANES, width), F32),
        ] + [pltpu.VMEM((ts, V7X_LANES), F32)] * ((2 * width + d) // V7X_LANES),
        compiler_params=_params(("arbitrary", "arbitrary")),
        name="rglru_mixer",
    )(x, g.reshape(1, d), w_in.astype(BF16), conv_w, conv_b.reshape(1, width), wax,
      b_a.reshape(1, width), b_x.reshape(1, width), lam.reshape(1, width), w_out.astype(BF16))


def _pa_in_kernel(x_ref, g_ref, wu_ref, wk_ref, wqt_ref, wvt_ref, pw_ref, ps_ref, qg_ref, kg_ref, e_ref,
                  p_ref, qt_ref, k_ref, vt_ref, km_ref, ucarry_scr):
    ts = x_ref.shape[1]
    nhp = qt_ref.shape[1]
    nbt = ts // MOBA_BLOCK
    s = pl.program_id(1)
    h = _rms(x_ref[0], g_ref[...]).astype(BF16)

    @pl.when(s == 0)
    def _():
        ucarry_scr[...] = jnp.zeros_like(ucarry_scr)

    u = jnp.dot(h, wu_ref[...], preferred_element_type=F32)
    halo = ucarry_scr.shape[0]
    ext = jnp.concatenate([ucarry_scr[...], u], axis=0)
    ucarry_scr[...] = u[ts - halo:, :]
    pos = s * ts + lax.broadcasted_iota(jnp.int32, (ts, 1), 0)
    for gi, w in enumerate(POOL_WINDOWS):
        lanes = slice(gi * POOL_GROUP, (gi + 1) * POOL_GROUP)
        acc = ext[:, lanes]
        sh = 1
        while sh < w:
            acc = acc + pltpu.roll(acc, sh, 0)
            sh *= 2
        cnt = jnp.minimum(pos + 1, w).astype(F32)
        pg = acc[halo:] / cnt - u[:, lanes]
        out = jnp.dot(pg.astype(BF16), pw_ref[gi], preferred_element_type=F32) * ps_ref[:, lanes]
        p_ref[0, :, lanes] = out.astype(BF16)

    nt = (((1,), (1,)), ((), ()))
    nh = wqt_ref.shape[0] // HEAD_DIM
    qt = lax.dot_general(wqt_ref[...], h, nt, preferred_element_type=F32)
    q3 = qt.reshape(nh, HEAD_DIM, ts)
    ms = jnp.sum(q3 * q3, axis=1, keepdims=True) * (1.0 / HEAD_DIM)
    qn = ((q3 * lax.rsqrt(ms + EPS)).reshape(nh * HEAD_DIM, ts) * qg_ref[...]).astype(BF16)
    vt = lax.dot_general(wvt_ref[...], h, nt, preferred_element_type=F32).astype(BF16)

    kk = jnp.dot(h, wk_ref[...], preferred_element_type=F32)
    ssq = jnp.dot((kk * kk).astype(BF16), e_ref[...], preferred_element_type=F32)
    kn = kk * lax.rsqrt(ssq * (1.0 / HEAD_DIM) + EPS) * kg_ref[...]
    knb = kn.astype(BF16)

    hpw = 2 * HEAD_DIM
    for jb in range(nbt):
        cols = slice(jb * MOBA_BLOCK, (jb + 1) * MOBA_BLOCK)
        kmean = jnp.sum(kn[cols, :], axis=0, keepdims=True) * (1.0 / MOBA_BLOCK)
        for hp in range(nhp):
            rows = slice(hp * hpw, (hp + 1) * hpw)
            qt_ref[0, hp, jb] = qn[rows, cols]
            vt_ref[0, hp, jb] = vt[rows, cols]
            k_ref[0, hp, jb] = knb[cols, rows]
            km_ref[0, hp, pl.ds(s * nbt + jb, 1), :] = kmean[:, rows]


def _pa_in(x, g, w_in, pool_w, pool_scale, q_gain, k_gain):
    b, s, d = x.shape
    pw = pool_w.shape[0] * POOL_GROUP
    aw = (w_in.shape[1] - pw) // 3
    nh = aw // HEAD_DIM
    nhp = nh // 2
    nb = s // MOBA_BLOCK
    ts = SEQ_TILE
    hpw = 2 * HEAD_DIM
    wu = w_in[:, :pw].astype(BF16)
    wqt = w_in[:, pw:pw + aw].T.astype(BF16)
    wk = w_in[:, pw + aw:pw + 2 * aw].astype(BF16)
    wvt = w_in[:, pw + 2 * aw:].T.astype(BF16)
    qg = (jnp.tile(q_gain, nh) * (HEAD_DIM ** -0.5)).reshape(aw, 1)
    kg = jnp.tile(k_gain, nh).reshape(1, aw)
    hid = jnp.arange(aw) // HEAD_DIM
    e = (hid[:, None] == hid[None, :]).astype(BF16)
    return pl.pallas_call(
        _pa_in_kernel,
        out_shape=(
            jax.ShapeDtypeStruct((b, s, pw), BF16),
            jax.ShapeDtypeStruct((b, nhp, nb, hpw, MOBA_BLOCK), BF16),
            jax.ShapeDtypeStruct((b, nhp, nb, MOBA_BLOCK, hpw), BF16),
            jax.ShapeDtypeStruct((b, nhp, nb, hpw, MOBA_BLOCK), BF16),
            jax.ShapeDtypeStruct((b, nhp, nb, hpw), F32),
        ),
        grid=(b, s // ts),
        in_specs=[
            pl.BlockSpec((1, ts, d), lambda i, j: (i, j, 0)),
            _const_spec((1, d)),
            _const_spec((d, pw)),
            _const_spec((d, aw)),
            _const_spec((aw, d)),
            _const_spec((aw, d)),
            _const_spec(pool_w.shape),
            _const_spec((1, pw)),
            _const_spec((aw, 1)),
            _const_spec((1, aw)),
            _const_spec((aw, aw)),
        ],
        out_specs=(
            pl.BlockSpec((1, ts, pw), lambda i, j: (i, j, 0)),
            pl.BlockSpec((1, nhp, ts // MOBA_BLOCK, hpw, MOBA_BLOCK), lambda i, j: (i, 0, j, 0, 0)),
            pl.BlockSpec((1, nhp, ts // MOBA_BLOCK, MOBA_BLOCK, hpw), lambda i, j: (i, 0, j, 0, 0)),
            pl.BlockSpec((1, nhp, ts // MOBA_BLOCK, hpw, MOBA_BLOCK), lambda i, j: (i, 0, j, 0, 0)),
            pl.BlockSpec((1, nhp, nb, hpw), lambda i, j: (i, 0, 0, 0)),
        ),
        scratch_shapes=[pltpu.VMEM((max(POOL_WINDOWS), pw), F32)],
        compiler_params=_params(("arbitrary", "arbitrary")),
        name="pool_attn_in",
    )(x, g.reshape(1, d), wu, wk, wqt, wvt, pool_w.astype(BF16), pool_scale.reshape(1, pw), qg, kg, e)


def _moba_kernel(qt_ref, k_ref, vt_ref, km_ref, bias_ref, jsl_ref, o_ref, pv_scr, m_scr, w_scr):
    nqb = qt_ref.shape[2]
    nb = k_ref.shape[2]
    hpw, bq = qt_ref.shape[3], qt_ref.shape[4]
    iq = [pl.program_id(2) * nqb + qi for qi in range(nqb)]
    hrow = lax.broadcasted_iota(jnp.int32, (hpw, bq), 0) < HEAD_DIM
    mine = (hrow, jnp.logical_not(hrow))
    jj = lax.broadcasted_iota(jnp.int32, (nb, bq), 0)
    kmb = km_ref[0, 0].astype(BF16)
    zeros = jnp.zeros((hpw, bq), BF16)
    qh = [[jnp.where(mine[hh], qt_ref[0, 0, qi], zeros) for hh in range(2)] for qi in range(nqb)]
    m_scr[...] = jnp.zeros_like(m_scr)

    for qi in range(nqb):
        for hh in range(2):
            gate = jnp.dot(kmb, qh[qi][hh], preferred_element_type=F32)
            gt = jnp.where(jj < iq[qi], gate, -jnp.inf)
            rank = jnp.zeros((nb, bq), jnp.int32)
            for m in range(nb):
                gm = gt[m:m + 1, :]
                ahead = (gm > gt) | ((gm == gt) & (jj > m))
                rank = rank + ahead.astype(jnp.int32)
            keep = ((jj < iq[qi]) & (rank < MOBA_TOPK)) | (jj == iq[qi])
            w_scr[qi, hh] = jnp.where(keep, 0.0, NEG_INF) + jsl_ref[hh]

    def block(j, own=None):
        kj = k_ref[0, 0, j]
        vj = vt_ref[0, 0, j]
        for hh in range(2):
            vh = jnp.where(mine[hh], vj, jnp.ones_like(vj))
            for qi in range(0 if own is None else own, nqb):
                kind = 1 if qi == own else 0
                for qs in range(0, bq, QUERY_SPLIT):
                    ql = slice(qs, qs + QUERY_SPLIT)
                    st = jnp.dot(kj, qh[qi][hh][:, ql], preferred_element_type=F32) + bias_ref[kind, hh, :, ql]
                    mx = jnp.max(st, axis=0, keepdims=True)
                    p = jnp.exp(st - mx).astype(BF16)
                    pv_scr[j, qi, hh, :, ql] = jnp.dot(vh, p, preferred_element_type=F32)
                    m_scr[qi, hh, qs // QUERY_SPLIT, pl.ds(j, 1), :] = mx

    def group(t, carry):
        for u in range(nqb):
            block(nqb * t + u)
        return carry

    lax.fori_loop(0, pl.program_id(2), group, 0)
    for u in range(nqb):
        block(iq[u], own=u)

    ext = HEAD_DIM + V7X_SUBLANES
    rows = (slice(0, ext), slice(hpw - ext, hpw))

    def comb(j, acc, qi, hh):
        return acc + pv_scr[j, qi, hh, rows[hh], :] * w_scr[qi, hh, pl.ds(j, 1), :]

    for qi in range(nqb):
        accs = []
        for hh in range(2):
            mr = jnp.concatenate([m_scr[qi, hh, t] for t in range(bq // QUERY_SPLIT)], axis=-1) + w_scr[qi, hh]
            w_scr[qi, hh] = jnp.exp(mr - jnp.max(mr, axis=0, keepdims=True))
            accs.append(lax.fori_loop(0, iq[qi] + 1, functools.partial(comb, qi=qi, hh=hh),
                                      jnp.zeros((ext, bq), F32)))
        ot = jnp.concatenate([accs[0][:HEAD_DIM] / accs[0][HEAD_DIM:HEAD_DIM + 1],
                              accs[1][V7X_SUBLANES:] / accs[1][0:1]], axis=0)
        o_ref[0, qi * bq:(qi + 1) * bq, :] = ot.T.astype(BF16)


def _moba(qt, k, vt, km):
    b, nhp, nb, hpw, bq = qt.shape
    nh = 2 * nhp
    nqb = Q_BLOCKS
    slopes = 2.0 ** (-8.0 * jnp.arange(1, nh + 1, dtype=F32) / nh)
    r = jnp.arange(MOBA_BLOCK, dtype=F32)
    past = slopes[:, None, None] * jnp.broadcast_to(r[:, None], (MOBA_BLOCK, bq))[None]
    causal = jnp.arange(MOBA_BLOCK)[:, None] <= jnp.arange(bq)[None, :]
    bias = jnp.stack([past, past + jnp.where(causal, 0.0, NEG_INF)[None]])
    jsl = slopes[:, None, None] * jnp.broadcast_to(
        (jnp.arange(nb, dtype=F32) * MOBA_BLOCK)[:, None], (nb, bq))[None]
    return pl.pallas_call(
        _moba_kernel,
        out_shape=jax.ShapeDtypeStruct((b, nb * bq, nhp * hpw), BF16),
        grid=(b, nhp, nb // nqb),
        in_specs=[
            pl.BlockSpec((1, 1, nqb, hpw, bq), lambda bi, hp, i: (bi, hp, i, 0, 0)),
            pl.BlockSpec((1, 1, nb, MOBA_BLOCK, hpw), lambda bi, hp, i: (bi, hp, 0, 0, 0)),
            pl.BlockSpec((1, 1, nb, hpw, MOBA_BLOCK), lambda bi, hp, i: (bi, hp, 0, 0, 0)),
            pl.BlockSpec((1, 1, nb, hpw), lambda bi, hp, i: (bi, hp, 0, 0)),
            pl.BlockSpec((2, 2, MOBA_BLOCK, bq), lambda bi, hp, i: (0, hp, 0, 0)),
            pl.BlockSpec((2, nb, bq), lambda bi, hp, i: (hp, 0, 0)),
        ],
        out_specs=pl.BlockSpec((1, nqb * bq, hpw), lambda bi, hp, i: (bi, i, hp)),
        scratch_shapes=[
            pltpu.VMEM((nb, nqb, 2, hpw, bq), F32),
            pltpu.VMEM((nqb, 2, bq // QUERY_SPLIT, nb, QUERY_SPLIT), F32),
            pltpu.VMEM((nqb, 2, nb, bq), F32),
        ],
        compiler_params=_params(("arbitrary", "arbitrary", "arbitrary")),
        name="moba_attention",
    )(qt, k, vt, km, bias, jsl)


def _pa_out_kernel(x_ref, p_ref, a_ref, wp_ref, wa_ref, o_ref):
    y = jnp.dot(p_ref[0], wp_ref[...], preferred_element_type=F32)
    y = y + jnp.dot(a_ref[0], wa_ref[...], preferred_element_type=F32)
    o_ref[0] = x_ref[0] + y


def _pa_out(x, p, a, w_out):
    b, s, d = x.shape
    pw, aw = p.shape[-1], a.shape[-1]
    ts = SEQ_TILE
    return pl.pallas_call(
        _pa_out_kernel,
        out_shape=jax.ShapeDtypeStruct((b, s, d), F32),
        grid=(b, s // ts),
        in_specs=[
            pl.BlockSpec((1, ts, d), lambda i, j: (i, j, 0)),
            pl.BlockSpec((1, ts, pw), lambda i, j: (i, j, 0)),
            pl.BlockSpec((1, ts, aw), lambda i, j: (i, j, 0)),
            _const_spec((pw, d)),
            _const_spec((aw, d)),
        ],
        out_specs=pl.BlockSpec((1, ts, d), lambda i, j: (i, j, 0)),
        compiler_params=_params(("arbitrary", "arbitrary")),
        name="pool_attn_out",
    )(x, p, a, w_out[:pw].astype(BF16), w_out[pw:].astype(BF16))


@jax.jit
def kernel(x, norm_mix, norm_ffn, pa_w_in, pa_pool_w, pa_pool_scale, pa_q_gain, pa_k_gain, pa_w_out,
           rg_w_in, rg_conv_w, rg_conv_b, rg_w_a, rg_b_a, rg_w_x, rg_b_x, rg_lambda, rg_w_out,
           ffn_w_in, ffn_conv_w, ffn_conv_b, ffn_w_out):
    depth = norm_mix.shape[0]
    for l in range(depth):
        j = l // 2
        if l % 2 == 0:
            p, qt, k, vt, km = _pa_in(x, norm_mix[l], pa_w_in[j], pa_pool_w[j], pa_pool_scale[j],
                                      pa_q_gain[j], pa_k_gain[j])
            a = _moba(qt, k, vt, km)
            x = _pa_out(x, p, a, pa_w_out[j])
        else:
            x = _rglru(x, norm_mix[l], rg_w_in[j], rg_conv_w[j], rg_conv_b[j], rg_w_a[j], rg_b_a[j],
                       rg_w_x[j], rg_b_x[j], rg_lambda[j], rg_w_out[j])
        x = _ffn(x, norm_ffn[l], ffn_w_in[l], ffn_conv_w[l], ffn_conv_b[l], ffn_w_out[l])
    return x
```

```python
import functools
import math

import jax
import jax.numpy as jnp
from jax import lax
from jax.experimental import pallas as pl
from jax.experimental.pallas import tpu as pltpu

F32 = jnp.float32
BF16 = jnp.bfloat16

EPS = 1e-6
NEG_INF = -1e30
POOL_WINDOWS = (2, 4, 8, 16)
POOL_GROUP = 128
HEAD_DIM = 64
MOBA_BLOCK = 256
MOBA_TOPK = 3
RG_C = 8.0
RNN_BLOCK = 128

V7X_LANES = 128
V7X_SUBLANES = 8
V7X_VMEM_LIMIT_BYTES = 56 * 1024 * 1024

SEQ_TILE = 512
FFN_CHUNK = 256
QUERY_SPLIT = 128
Q_BLOCKS = 4


def _rms(x, g):
    ms = jnp.mean(x * x, axis=-1, keepdims=True)
    return x * lax.rsqrt(ms + EPS) * g


_GELU_K1 = -2.0 * math.sqrt(2.0 / math.pi) * math.log2(math.e)
_GELU_K2 = _GELU_K1 * 0.044715


def _gelu(x):
    return x / (1.0 + jnp.exp2((x * x * _GELU_K2 + _GELU_K1) * x))


def _const_spec(shape):
    zeros = (0,) * len(shape)
    return pl.BlockSpec(shape, lambda *_: zeros, pipeline_mode=pl.Buffered(1))


def _params(sem):
    return pltpu.CompilerParams(dimension_semantics=sem, vmem_limit_bytes=V7X_VMEM_LIMIT_BYTES)


def _segment_rows(gi, seg):
    s, p = divmod(V7X_SUBLANES * gi, seg)
    return pl.ds(p * V7X_SUBLANES + s, V7X_SUBLANES, stride=V7X_SUBLANES)


def _conv_segment_major(zp, prev_tail, w, bias):
    kw, hl, ts = w.shape[0], V7X_SUBLANES, zp.shape[0]
    sub = lax.broadcasted_iota(jnp.int32, (hl, 1), 0)
    tail = zp[ts - (kw - 1) * hl:]
    halos = []
    for i in range(kw - 1):
        grp = slice(i * hl, (i + 1) * hl)
        halos.append(pltpu.roll(jnp.where(sub == hl - 1, prev_tail[grp], tail[grp]), 1, 0))
    ext = jnp.concatenate(halos + [zp], axis=0)
    y = bias
    for k in range(kw):
        y = y + ext[k * hl:k * hl + ts] * w[k:k + 1]
    return y, tail


def _ffn_kernel(*refs, nch, cw, nres):
    x_ref, refs = refs[0], refs[1:]
    acts, wts, refs = refs[:nres], refs[nres:2 * nres], refs[2 * nres:]
    g_ref, wug_ref, cw_ref, cb_ref, wo_ref, o_ref, h_scr, carry_scr, z0_scr, z1_scr = refs
    z_scr = (z0_scr, z1_scr)
    ts = x_ref.shape[1]
    kw = cw_ref.shape[1]
    hl = V7X_SUBLANES
    x = x_ref[0]
    for a_ref, w_ref in zip(acts, wts):
        x = x + jnp.dot(a_ref[0], w_ref[...], preferred_element_type=F32)
    h_scr[...] = _rms(x, g_ref[...]).astype(BF16)
    o_ref[0] = x

    @pl.when(pl.program_id(1) == 0)
    def _():
        carry_scr[...] = jnp.zeros_like(carry_scr)

    def up(c, slot):
        z_scr[slot][hl:, :] = jnp.dot(h_scr[...], wug_ref[c], preferred_element_type=F32)

    def down(c, slot):
        zb = z_scr[slot]
        zb[0:hl, :] = carry_scr[c]
        carry_scr[c] = zb[ts:ts + hl, :]
        w = cw_ref[c]
        y = cb_ref[c] + zb[hl:, :] * w[kw - 1:kw]
        for k in range(kw - 1):
            off = hl - (kw - 1) + k
            y = y + zb[off:off + ts, :] * w[k:k + 1]
        a = (_gelu(y[:, cw:]) * y[:, :cw]).astype(BF16)
        o_ref[0] += jnp.dot(a, wo_ref[c], preferred_element_type=F32)

    up(0, 0)

    def pair(q, carry):
        c = 2 * q
        up(c + 1, 1)
        down(c, 0)
        up(c + 2, 0)
        down(c + 1, 1)
        return carry

    npairs = (nch - 1) // 2
    lax.fori_loop(0, npairs, pair, 0)
    if nch % 2 == 0:
        up(nch - 1, 1)
        down(nch - 2, 0)
        down(nch - 1, 1)
    else:
        down(nch - 1, 0)


def _ffn(x, g, w_in, conv_w, conv_b, w_out, residual=()):
    b, s, d = x.shape
    dff = w_out.shape[0]
    cw = FFN_CHUNK
    nch = dff // cw
    ts = SEQ_TILE
    wu = w_in[:, :dff].reshape(d, nch, cw)
    wg = w_in[:, dff:].reshape(d, nch, cw)
    wug = jnp.concatenate([wu, wg], axis=-1).transpose(1, 0, 2).astype(BF16)
    cwu = conv_w[:, :dff].reshape(-1, nch, cw)
    cwg = conv_w[:, dff:].reshape(-1, nch, cw)
    cwr = jnp.concatenate([cwu, cwg], axis=-1).transpose(1, 0, 2)
    cbr = jnp.concatenate([conv_b[:dff].reshape(nch, 1, cw), conv_b[dff:].reshape(nch, 1, cw)], axis=-1)
    wo = w_out.reshape(nch, cw, d).astype(BF16)
    kern = functools.partial(_ffn_kernel, nch=nch, cw=cw, nres=len(residual))
    acts = [a for a, _ in residual]
    wts = [w.astype(BF16) for _, w in residual]
    return pl.pallas_call(
        kern,
        out_shape=jax.ShapeDtypeStruct((b, s, d), F32),
        grid=(b, s // ts),
        in_specs=[pl.BlockSpec((1, ts, d), lambda i, j: (i, j, 0))]
        + [pl.BlockSpec((1, ts, a.shape[-1]), lambda i, j: (i, j, 0)) for a in acts]
        + [_const_spec(w.shape) for w in wts]
        + [
            _const_spec((1, d)),
            _const_spec((nch, d, 2 * cw)),
            _const_spec((nch, conv_w.shape[0], 2 * cw)),
            _const_spec((nch, 1, 2 * cw)),
            _const_spec((nch, cw, d)),
        ],
        out_specs=pl.BlockSpec((1, ts, d), lambda i, j: (i, j, 0)),
        scratch_shapes=[
            pltpu.VMEM((ts, d), BF16),
            pltpu.VMEM((nch, V7X_SUBLANES, 2 * cw), F32),
            pltpu.VMEM((V7X_SUBLANES + ts, 2 * cw), F32),
            pltpu.VMEM((V7X_SUBLANES + ts, 2 * cw), F32),
        ],
        compiler_params=_params(("arbitrary", "arbitrary")),
        name="conv_ffn",
    )(x, *acts, *wts, g.reshape(1, d), wug, cwr, cbr, wo)


def _rglru_kernel(x_ref, g_ref, win_ref, cw_ref, cb_ref, wax_ref, ba_ref, bx_ref, lam_ref, wo_ref,
                  o_ref, xcarry_scr, hcarry_scr, *slabs, width, nblk):
    ts = x_ref.shape[1]
    hl = V7X_SUBLANES
    seg = ts // hl
    nw = width // V7X_LANES
    x = x_ref[0]
    h = _rms(x, g_ref[...]).astype(BF16)

    @pl.when(pl.program_id(1) == 0)
    def _():
        xcarry_scr[...] = jnp.zeros_like(xcarry_scr)
        hcarry_scr[...] = jnp.zeros_like(hcarry_scr)

    z = jnp.dot(h, win_ref[...], preferred_element_type=F32)
    gate = z[:, :width]
    for gi in range(seg):
        t0 = hl * gi
        for l in range(nw):
            lanes = slice(width + l * V7X_LANES, width + (l + 1) * V7X_LANES)
            slabs[l][_segment_rows(gi, seg), :] = z[t0:t0 + hl, lanes]
    xr = jnp.concatenate([slabs[l][...] for l in range(nw)], axis=-1)
    xc, tail = _conv_segment_major(xr, xcarry_scr[...], cw_ref[...], cb_ref[...])
    xcarry_scr[...] = tail

    xcb = xc.astype(BF16)
    ra, rx = [], []
    for n in range(nblk):
        ax = jnp.dot(xcb[:, n * RNN_BLOCK:(n + 1) * RNN_BLOCK], wax_ref[n], preferred_element_type=F32)
        ra.append(ax[:, :RNN_BLOCK])
        rx.append(ax[:, RNN_BLOCK:])
    r = jax.nn.sigmoid(jnp.concatenate(ra, axis=-1) + ba_ref[...])
    ig = jax.nn.sigmoid(jnp.concatenate(rx, axis=-1) + bx_ref[...])
    log_a = (-RG_C * jax.nn.softplus(-lam_ref[...])) * r
    a = jnp.exp(log_a)
    om = 1.0 - a * a
    bb = jnp.where(om > 0.0, om * lax.rsqrt(om), 0.0) * (ig * xc)

    a3 = a.reshape(seg, hl, width)
    b3 = bb.reshape(seg, hl, width)
    hloc, ploc = [b3[0]], [a3[0]]
    for p in range(1, seg):
        hloc.append(a3[p] * hloc[-1] + b3[p])
        ploc.append(a3[p] * ploc[-1])
    sub = lax.broadcasted_iota(jnp.int32, (hl, 1), 0)
    base = pltpu.roll(hcarry_scr[...], 1, 0)
    enter = base
    for _ in range(hl - 1):
        enter = jnp.where(sub == 0, base, pltpu.roll(hloc[-1] + ploc[-1] * enter, 1, 0))
    hcarry_scr[...] = hloc[-1] + ploc[-1] * enter
    for p in range(seg):
        hp = hloc[p] + ploc[p] * enter
        for l in range(nw):
            slabs[l][p * hl:(p + 1) * hl, :] = hp[:, l * V7X_LANES:(l + 1) * V7X_LANES]
    hs = jnp.concatenate(
        [jnp.concatenate([slabs[l][_segment_rows(gi, seg), :] for l in range(nw)], axis=-1)
         for gi in range(seg)], axis=0)

    y = (_gelu(gate) * hs).astype(BF16)
    o_ref[0] = x + jnp.dot(y, wo_ref[...], preferred_element_type=F32)


def _rglru(x, g, w_in, conv_w, conv_b, w_a, b_a, w_x, b_x, lam, w_out):
    b, s, d = x.shape
    width = w_out.shape[0]
    nblk = w_a.shape[0]
    ts = SEQ_TILE
    wax = jnp.concatenate([w_a, w_x], axis=-1).astype(BF16)
    kern = functools.partial(_rglru_kernel, width=width, nblk=nblk)
    return pl.pallas_call(
        kern,
        out_shape=jax.ShapeDtypeStruct((b, s, d), F32),
        grid=(b, s // ts),
        in_specs=[
            pl.BlockSpec((1, ts, d), lambda i, j: (i, j, 0)),
            _const_spec((1, d)),
            _const_spec((d, 2 * width)),
            _const_spec(conv_w.shape),
            _const_spec((1, width)),
            _const_spec(wax.shape),
            _const_spec((1, width)),
            _const_spec((1, width)),
            _const_spec((1, width)),
            _const_spec((width, d)),
        ],
        out_specs=pl.BlockSpec((1, ts, d), lambda i, j: (i, j, 0)),
        scratch_shapes=[
            pltpu.VMEM(((conv_w.shape[0] - 1) * V7X_SUBLANES, width), F32),
            pltpu.VMEM((V7X_SUBLANES, width), F32),
        ] + [pltpu.VMEM((ts, V7X_LANES), F32)] * (width // V7X_LANES),
        compiler_params=_params(("arbitrary", "arbitrary")),
        name="rglru_mixer",
    )(x, g.reshape(1, d), w_in.astype(BF16), conv_w, conv_b.reshape(1, width), wax,
      b_a.reshape(1, width), b_x.reshape(1, width), lam.reshape(1, width), w_out.astype(BF16))


def _pa_in_kernel(x_ref, g_ref, wu_ref, wk_ref, wqt_ref, wvt_ref, pw_ref, ps_ref, qg_ref, kg_ref, e_ref,
                  p_ref, qt_ref, k_ref, vt_ref, km_ref, ucarry_scr):
    ts = x_ref.shape[1]
    nhp = qt_ref.shape[1]
    nbt = ts // MOBA_BLOCK
    s = pl.program_id(1)
    h = _rms(x_ref[0], g_ref[...]).astype(BF16)

    @pl.when(s == 0)
    def _():
        ucarry_scr[...] = jnp.zeros_like(ucarry_scr)

    u = jnp.dot(h, wu_ref[...], preferred_element_type=F32)
    halo = ucarry_scr.shape[0]
    ext = jnp.concatenate([ucarry_scr[...], u], axis=0)
    ucarry_scr[...] = u[ts - halo:, :]
    pos = s * ts + lax.broadcasted_iota(jnp.int32, (ts, 1), 0)
    for gi, w in enumerate(POOL_WINDOWS):
        lanes = slice(gi * POOL_GROUP, (gi + 1) * POOL_GROUP)
        acc = ext[:, lanes]
        sh = 1
        while sh < w:
            acc = acc + pltpu.roll(acc, sh, 0)
            sh *= 2
        cnt = jnp.minimum(pos + 1, w).astype(F32)
        pg = acc[halo:] / cnt - u[:, lanes]
        out = jnp.dot(pg.astype(BF16), pw_ref[gi], preferred_element_type=F32) * ps_ref[:, lanes]
        p_ref[0, :, lanes] = out.astype(BF16)

    nt = (((1,), (1,)), ((), ()))
    nh = wqt_ref.shape[0] // HEAD_DIM
    qt = lax.dot_general(wqt_ref[...], h, nt, preferred_element_type=F32)
    q3 = qt.reshape(nh, HEAD_DIM, ts)
    ms = jnp.sum(q3 * q3, axis=1, keepdims=True) * (1.0 / HEAD_DIM)
    qn = ((q3 * lax.rsqrt(ms + EPS)).reshape(nh * HEAD_DIM, ts) * qg_ref[...]).astype(BF16)
    vt = lax.dot_general(wvt_ref[...], h, nt, preferred_element_type=F32).astype(BF16)

    kk = jnp.dot(h, wk_ref[...], preferred_element_type=F32)
    ssq = jnp.dot((kk * kk).astype(BF16), e_ref[...], preferred_element_type=F32)
    kn = kk * lax.rsqrt(ssq * (1.0 / HEAD_DIM) + EPS) * kg_ref[...]
    knb = kn.astype(BF16)

    hpw = 2 * HEAD_DIM
    for jb in range(nbt):
        cols = slice(jb * MOBA_BLOCK, (jb + 1) * MOBA_BLOCK)
        kmean = jnp.sum(kn[cols, :], axis=0, keepdims=True) * (1.0 / MOBA_BLOCK)
        for hp in range(nhp):
            rows = slice(hp * hpw, (hp + 1) * hpw)
            qt_ref[0, hp, jb] = qn[rows, cols]
            vt_ref[0, hp, jb] = vt[rows, cols]
            k_ref[0, hp, jb] = knb[cols, rows]
            km_ref[0, hp, pl.ds(s * nbt + jb, 1), :] = kmean[:, rows]


def _pa_in(x, g, w_in, pool_w, pool_scale, q_gain, k_gain):
    b, s, d = x.shape
    pw = pool_w.shape[0] * POOL_GROUP
    aw = (w_in.shape[1] - pw) // 3
    nh = aw // HEAD_DIM
    nhp = nh // 2
    nb = s // MOBA_BLOCK
    ts = SEQ_TILE
    hpw = 2 * HEAD_DIM
    wu = w_in[:, :pw].astype(BF16)
    wqt = w_in[:, pw:pw + aw].T.astype(BF16)
    wk = w_in[:, pw + aw:pw + 2 * aw].astype(BF16)
    wvt = w_in[:, pw + 2 * aw:].T.astype(BF16)
    qg = (jnp.tile(q_gain, nh) * (HEAD_DIM ** -0.5)).reshape(aw, 1)
    kg = jnp.tile(k_gain, nh).reshape(1, aw)
    hid = jnp.arange(aw) // HEAD_DIM
    e = (hid[:, None] == hid[None, :]).astype(BF16)
    return pl.pallas_call(
        _pa_in_kernel,
        out_shape=(
            jax.ShapeDtypeStruct((b, s, pw), BF16),
            jax.ShapeDtypeStruct((b, nhp, nb, hpw, MOBA_BLOCK), BF16),
            jax.ShapeDtypeStruct((b, nhp, nb, MOBA_BLOCK, hpw), BF16),
            jax.ShapeDtypeStruct((b, nhp, nb, hpw, MOBA_BLOCK), BF16),
            jax.ShapeDtypeStruct((b, nhp, nb, hpw), F32),
        ),
        grid=(b, s // ts),
        in_specs=[
            pl.BlockSpec((1, ts, d), lambda i, j: (i, j, 0)),
            _const_spec((1, d)),
            _const_spec((d, pw)),
            _const_spec((d, aw)),
            _const_spec((aw, d)),
            _const_spec((aw, d)),
            _const_spec(pool_w.shape),
            _const_spec((1, pw)),
            _const_spec((aw, 1)),
            _const_spec((1, aw)),
            _const_spec((aw, aw)),
        ],
        out_specs=(
            pl.BlockSpec((1, ts, pw), lambda i, j: (i, j, 0)),
            pl.BlockSpec((1, nhp, ts // MOBA_BLOCK, hpw, MOBA_BLOCK), lambda i, j: (i, 0, j, 0, 0)),
            pl.BlockSpec((1, nhp, ts // MOBA_BLOCK, MOBA_BLOCK, hpw), lambda i, j: (i, 0, j, 0, 0)),
            pl.BlockSpec((1, nhp, ts // MOBA_BLOCK, hpw, MOBA_BLOCK), lambda i, j: (i, 0, j, 0, 0)),
            pl.BlockSpec((1, nhp, nb, hpw), lambda i, j: (i, 0, 0, 0)),
        ),
        scratch_shapes=[pltpu.VMEM((max(POOL_WINDOWS), pw), F32)],
        compiler_params=_params(("arbitrary", "arbitrary")),
        name="pool_attn_in",
    )(x, g.reshape(1, d), wu, wk, wqt, wvt, pool_w.astype(BF16), pool_scale.reshape(1, pw), qg, kg, e)


def _moba_kernel(qt_ref, k_ref, vt_ref, km_ref, bias_ref, jsl_ref, o_ref, pv_scr, m_scr, w_scr):
    nqb = qt_ref.shape[2]
    nb = k_ref.shape[2]
    hpw, bq = qt_ref.shape[3], qt_ref.shape[4]
    iq = [pl.program_id(2) * nqb + qi for qi in range(nqb)]
    hrow = lax.broadcasted_iota(jnp.int32, (hpw, bq), 0) < HEAD_DIM
    mine = (hrow, jnp.logical_not(hrow))
    jj = lax.broadcasted_iota(jnp.int32, (nb, bq), 0)
    kmb = km_ref[0, 0].astype(BF16)
    zeros = jnp.zeros((hpw, bq), BF16)
    qh = [[jnp.where(mine[hh], qt_ref[0, 0, qi], zeros) for hh in range(2)] for qi in range(nqb)]
    m_scr[...] = jnp.zeros_like(m_scr)

    for qi in range(nqb):
        for hh in range(2):
            gate = jnp.dot(kmb, qh[qi][hh], preferred_element_type=F32)
            gt = jnp.where(jj < iq[qi], gate, -jnp.inf)
            rank = jnp.zeros((nb, bq), jnp.int32)
            for m in range(nb):
                gm = gt[m:m + 1, :]
                ahead = (gm > gt) | ((gm == gt) & (jj > m))
                rank = rank + ahead.astype(jnp.int32)
            keep = ((jj < iq[qi]) & (rank < MOBA_TOPK)) | (jj == iq[qi])
            w_scr[qi, hh] = jnp.where(keep, 0.0, NEG_INF) + jsl_ref[hh]

    def block(j, own=None):
        kj = k_ref[0, 0, j]
        vj = vt_ref[0, 0, j]
        for hh in range(2):
            vh = jnp.where(mine[hh], vj, jnp.ones_like(vj))
            for qi in range(0 if own is None else own, nqb):
                kind = 1 if qi == own else 0
                for qs in range(0, bq, QUERY_SPLIT):
                    ql = slice(qs, qs + QUERY_SPLIT)
                    st = jnp.dot(kj, qh[qi][hh][:, ql], preferred_element_type=F32) + bias_ref[kind, hh, :, ql]
                    mx = jnp.max(st, axis=0, keepdims=True)
                    p = jnp.exp(st - mx).astype(BF16)
                    pv_scr[j, qi, hh, :, ql] = jnp.dot(vh, p, preferred_element_type=F32)
                    m_scr[qi, hh, qs // QUERY_SPLIT, pl.ds(j, 1), :] = mx

    def group(t, carry):
        for u in range(nqb):
            block(nqb * t + u)
        return carry

    lax.fori_loop(0, pl.program_id(2), group, 0)
    for u in range(nqb):
        block(iq[u], own=u)

    ext = HEAD_DIM + V7X_SUBLANES
    rows = (slice(0, ext), slice(hpw - ext, hpw))

    def comb(j, acc, qi, hh):
        return acc + pv_scr[j, qi, hh, rows[hh], :] * w_scr[qi, hh, pl.ds(j, 1), :]

    for qi in range(nqb):
        accs = []
        for hh in range(2):
            mr = jnp.concatenate([m_scr[qi, hh, t] for t in range(bq // QUERY_SPLIT)], axis=-1) + w_scr[qi, hh]
            w_scr[qi, hh] = jnp.exp(mr - jnp.max(mr, axis=0, keepdims=True))
            accs.append(lax.fori_loop(0, iq[qi] + 1, functools.partial(comb, qi=qi, hh=hh),
                                      jnp.zeros((ext, bq), F32)))
        ot = jnp.concatenate([accs[0][:HEAD_DIM] / accs[0][HEAD_DIM:HEAD_DIM + 1],
                              accs[1][V7X_SUBLANES:] / accs[1][0:1]], axis=0)
        o_ref[0, qi * bq:(qi + 1) * bq, :] = ot.T.astype(BF16)


def _moba(qt, k, vt, km):
    b, nhp, nb, hpw, bq = qt.shape
    nh = 2 * nhp
    nqb = Q_BLOCKS
    slopes = 2.0 ** (-8.0 * jnp.arange(1, nh + 1, dtype=F32) / nh)
    r = jnp.arange(MOBA_BLOCK, dtype=F32)
    past = slopes[:, None, None] * jnp.broadcast_to(r[:, None], (MOBA_BLOCK, bq))[None]
    causal = jnp.arange(MOBA_BLOCK)[:, None] <= jnp.arange(bq)[None, :]
    bias = jnp.stack([past, past + jnp.where(causal, 0.0, NEG_INF)[None]])
    jsl = slopes[:, None, None] * jnp.broadcast_to(
        (jnp.arange(nb, dtype=F32) * MOBA_BLOCK)[:, None], (nb, bq))[None]
    return pl.pallas_call(
        _moba_kernel,
        out_shape=jax.ShapeDtypeStruct((b, nb * bq, nhp * hpw), BF16),
        grid=(b, nhp, nb // nqb),
        in_specs=[
            pl.BlockSpec((1, 1, nqb, hpw, bq), lambda bi, hp, i: (bi, hp, i, 0, 0)),
            pl.BlockSpec((1, 1, nb, MOBA_BLOCK, hpw), lambda bi, hp, i: (bi, hp, 0, 0, 0)),
            pl.BlockSpec((1, 1, nb, hpw, MOBA_BLOCK), lambda bi, hp, i: (bi, hp, 0, 0, 0)),
            pl.BlockSpec((1, 1, nb, hpw), lambda bi, hp, i: (bi, hp, 0, 0)),
            pl.BlockSpec((2, 2, MOBA_BLOCK, bq), lambda bi, hp, i: (0, hp, 0, 0)),
            pl.BlockSpec((2, nb, bq), lambda bi, hp, i: (hp, 0, 0)),
        ],
        out_specs=pl.BlockSpec((1, nqb * bq, hpw), lambda bi, hp, i: (bi, i, hp)),
        scratch_shapes=[
            pltpu.VMEM((nb, nqb, 2, hpw, bq), F32),
            pltpu.VMEM((nqb, 2, bq // QUERY_SPLIT, nb, QUERY_SPLIT), F32),
            pltpu.VMEM((nqb, 2, nb, bq), F32),
        ],
        compiler_params=_params(("arbitrary", "arbitrary", "arbitrary")),
        name="moba_attention",
    )(qt, k, vt, km, bias, jsl)


@jax.jit
def kernel(x, norm_mix, norm_ffn, pa_w_in, pa_pool_w, pa_pool_scale, pa_q_gain, pa_k_gain, pa_w_out,
           rg_w_in, rg_conv_w, rg_conv_b, rg_w_a, rg_b_a, rg_w_x, rg_b_x, rg_lambda, rg_w_out,
           ffn_w_in, ffn_conv_w, ffn_conv_b, ffn_w_out):
    depth = norm_mix.shape[0]
    for l in range(depth):
        j = l // 2
        residual = ()
        if l % 2 == 0:
            p, qt, k, vt, km = _pa_in(x, norm_mix[l], pa_w_in[j], pa_pool_w[j], pa_pool_scale[j],
                                      pa_q_gain[j], pa_k_gain[j])
            a = _moba(qt, k, vt, km)
            pw = p.shape[-1]
            residual = ((p, pa_w_out[j][:pw]), (a, pa_w_out[j][pw:]))
        else:
            x = _rglru(x, norm_mix[l], rg_w_in[j], rg_conv_w[j], rg_conv_b[j], rg_w_a[j], rg_b_a[j],
                       rg_w_x[j], rg_b_x[j], rg_lambda[j], rg_w_out[j])
        x = _ffn(x, norm_ffn[l], ffn_w_in[l], ffn_conv_w[l], ffn_conv_b[l], ffn_w_out[l], residual)
    return x
```

```python
import functools
import math

import jax
import jax.numpy as jnp
from jax import lax
from jax.experimental import pallas as pl
from jax.experimental.pallas import tpu as pltpu

F32 = jnp.float32
BF16 = jnp.bfloat16

EPS = 1e-6
NEG_INF = -1e30
POOL_WINDOWS = (2, 4, 8, 16)
POOL_GROUP = 128
HEAD_DIM = 64
MOBA_BLOCK = 256
MOBA_TOPK = 3
RG_C = 8.0
RNN_BLOCK = 128

V7X_LANES = 128
V7X_SUBLANES = 8
V7X_VMEM_LIMIT_BYTES = 56 * 1024 * 1024

SEQ_TILE = 512
FFN_CHUNK = 256
QUERY_SPLIT = 128
Q_BLOCKS = 8


def _rms(x, g):
    ms = jnp.mean(x * x, axis=-1, keepdims=True)
    return x * lax.rsqrt(ms + EPS) * g


_GELU_K1 = -2.0 * math.sqrt(2.0 / math.pi) * math.log2(math.e)
_GELU_K2 = _GELU_K1 * 0.044715


def _gelu(x):
    return x / (1.0 + jnp.exp2((x * x * _GELU_K2 + _GELU_K1) * x))


def _const_spec(shape):
    zeros = (0,) * len(shape)
    return pl.BlockSpec(shape, lambda *_: zeros, pipeline_mode=pl.Buffered(1))


def _params(sem):
    return pltpu.CompilerParams(dimension_semantics=sem, vmem_limit_bytes=V7X_VMEM_LIMIT_BYTES)


def _segment_rows(gi, seg):
    s, p = divmod(V7X_SUBLANES * gi, seg)
    return pl.ds(p * V7X_SUBLANES + s, V7X_SUBLANES, stride=V7X_SUBLANES)


def _conv_segment_major(zp, prev_tail, w, bias):
    kw, hl, ts = w.shape[0], V7X_SUBLANES, zp.shape[0]
    sub = lax.broadcasted_iota(jnp.int32, (hl, 1), 0)
    tail = zp[ts - (kw - 1) * hl:]
    halos = []
    for i in range(kw - 1):
        grp = slice(i * hl, (i + 1) * hl)
        halos.append(pltpu.roll(jnp.where(sub == hl - 1, prev_tail[grp], tail[grp]), 1, 0))
    ext = jnp.concatenate(halos + [zp], axis=0)
    y = bias
    for k in range(kw):
        y = y + ext[k * hl:k * hl + ts] * w[k:k + 1]
    return y, tail


def _ffn_kernel(*refs, nch, cw, nres):
    x_ref, refs = refs[0], refs[1:]
    acts, wts, refs = refs[:nres], refs[nres:2 * nres], refs[2 * nres:]
    g_ref, wug_ref, cw_ref, cb_ref, wo_ref, o_ref, h_scr, carry_scr, z0_scr, z1_scr = refs
    z_scr = (z0_scr, z1_scr)
    ts = x_ref.shape[1]
    kw = cw_ref.shape[1]
    hl = V7X_SUBLANES
    x = x_ref[0]
    for a_ref, w_ref in zip(acts, wts):
        x = x + jnp.dot(a_ref[0], w_ref[...], preferred_element_type=F32)
    h_scr[...] = _rms(x, g_ref[...]).astype(BF16)
    o_ref[0] = x

    @pl.when(pl.program_id(1) == 0)
    def _():
        carry_scr[...] = jnp.zeros_like(carry_scr)

    def up(c, slot):
        z_scr[slot][hl:, :] = jnp.dot(h_scr[...], wug_ref[c], preferred_element_type=F32)

    def down(c, slot):
        zb = z_scr[slot]
        zb[0:hl, :] = carry_scr[c]
        carry_scr[c] = zb[ts:ts + hl, :]
        w = cw_ref[c]
        y = cb_ref[c] + zb[hl:, :] * w[kw - 1:kw]
        for k in range(kw - 1):
            off = hl - (kw - 1) + k
            y = y + zb[off:off + ts, :] * w[k:k + 1]
        a = (_gelu(y[:, cw:]) * y[:, :cw]).astype(BF16)
        o_ref[0] += jnp.dot(a, wo_ref[c], preferred_element_type=F32)

    up(0, 0)

    def pair(q, carry):
        c = 2 * q
        up(c + 1, 1)
        down(c, 0)
        up(c + 2, 0)
        down(c + 1, 1)
        return carry

    npairs = (nch - 1) // 2
    lax.fori_loop(0, npairs, pair, 0)
    if nch % 2 == 0:
        up(nch - 1, 1)
        down(nch - 2, 0)
        down(nch - 1, 1)
    else:
        down(nch - 1, 0)


def _ffn(x, g, w_in, conv_w, conv_b, w_out, residual=()):
    b, s, d = x.shape
    dff = w_out.shape[0]
    cw = FFN_CHUNK
    nch = dff // cw
    ts = SEQ_TILE
    wu = w_in[:, :dff].reshape(d, nch, cw)
    wg = w_in[:, dff:].reshape(d, nch, cw)
    wug = jnp.concatenate([wu, wg], axis=-1).transpose(1, 0, 2).astype(BF16)
    cwu = conv_w[:, :dff].reshape(-1, nch, cw)
    cwg = conv_w[:, dff:].reshape(-1, nch, cw)
    cwr = jnp.concatenate([cwu, cwg], axis=-1).transpose(1, 0, 2)
    cbr = jnp.concatenate([conv_b[:dff].reshape(nch, 1, cw), conv_b[dff:].reshape(nch, 1, cw)], axis=-1)
    wo = w_out.reshape(nch, cw, d).astype(BF16)
    kern = functools.partial(_ffn_kernel, nch=nch, cw=cw, nres=len(residual))
    acts = [a for a, _ in residual]
    wts = [w.astype(BF16) for _, w in residual]
    return pl.pallas_call(
        kern,
        out_shape=jax.ShapeDtypeStruct((b, s, d), F32),
        grid=(b, s // ts),
        in_specs=[pl.BlockSpec((1, ts, d), lambda i, j: (i, j, 0))]
        + [pl.BlockSpec((1, ts, a.shape[-1]), lambda i, j: (i, j, 0)) for a in acts]
        + [_const_spec(w.shape) for w in wts]
        + [
            _const_spec((1, d)),
            _const_spec((nch, d, 2 * cw)),
            _const_spec((nch, conv_w.shape[0], 2 * cw)),
            _const_spec((nch, 1, 2 * cw)),
            _const_spec((nch, cw, d)),
        ],
        out_specs=pl.BlockSpec((1, ts, d), lambda i, j: (i, j, 0)),
        scratch_shapes=[
            pltpu.VMEM((ts, d), BF16),
            pltpu.VMEM((nch, V7X_SUBLANES, 2 * cw), F32),
            pltpu.VMEM((V7X_SUBLANES + ts, 2 * cw), F32),
            pltpu.VMEM((V7X_SUBLANES + ts, 2 * cw), F32),
        ],
        compiler_params=_params(("arbitrary", "arbitrary")),
        name="conv_ffn",
    )(x, *acts, *wts, g.reshape(1, d), wug, cwr, cbr, wo)


def _rglru_kernel(x_ref, g_ref, win_ref, cw_ref, cb_ref, wax_ref, ba_ref, bx_ref, lam_ref, wo_ref,
                  o_ref, xcarry_scr, hcarry_scr, *slabs, width, nblk):
    ts = x_ref.shape[1]
    hl = V7X_SUBLANES
    seg = ts // hl
    nw = width // V7X_LANES
    x = x_ref[0]
    h = _rms(x, g_ref[...]).astype(BF16)

    @pl.when(pl.program_id(1) == 0)
    def _():
        xcarry_scr[...] = jnp.zeros_like(xcarry_scr)
        hcarry_scr[...] = jnp.zeros_like(hcarry_scr)

    z = jnp.dot(h, win_ref[...], preferred_element_type=F32)
    gate = z[:, :width]
    for gi in range(seg):
        t0 = hl * gi
        for l in range(nw):
            lanes = slice(width + l * V7X_LANES, width + (l + 1) * V7X_LANES)
            slabs[l][_segment_rows(gi, seg), :] = z[t0:t0 + hl, lanes]
    xr = jnp.concatenate([slabs[l][...] for l in range(nw)], axis=-1)
    xc, tail = _conv_segment_major(xr, xcarry_scr[...], cw_ref[...], cb_ref[...])
    xcarry_scr[...] = tail

    xcb = xc.astype(BF16)
    ra, rx = [], []
    for n in range(nblk):
        ax = jnp.dot(xcb[:, n * RNN_BLOCK:(n + 1) * RNN_BLOCK], wax_ref[n], preferred_element_type=F32)
        ra.append(ax[:, :RNN_BLOCK])
        rx.append(ax[:, RNN_BLOCK:])
    r = jax.nn.sigmoid(jnp.concatenate(ra, axis=-1) + ba_ref[...])
    ig = jax.nn.sigmoid(jnp.concatenate(rx, axis=-1) + bx_ref[...])
    log_a = (-RG_C * jax.nn.softplus(-lam_ref[...])) * r
    a = jnp.exp(log_a)
    om = 1.0 - a * a
    bb = jnp.where(om > 0.0, om * lax.rsqrt(om), 0.0) * (ig * xc)

    a3 = a.reshape(seg, hl, width)
    b3 = bb.reshape(seg, hl, width)
    hloc, ploc = [b3[0]], [a3[0]]
    for p in range(1, seg):
        hloc.append(a3[p] * hloc[-1] + b3[p])
        ploc.append(a3[p] * ploc[-1])
    sub = lax.broadcasted_iota(jnp.int32, (hl, 1), 0)
    base = pltpu.roll(hcarry_scr[...], 1, 0)
    enter = base
    for _ in range(hl - 1):
        enter = jnp.where(sub == 0, base, pltpu.roll(hloc[-1] + ploc[-1] * enter, 1, 0))
    hcarry_scr[...] = hloc[-1] + ploc[-1] * enter
    for p in range(seg):
        hp = hloc[p] + ploc[p] * enter
        for l in range(nw):
            slabs[l][p * hl:(p + 1) * hl, :] = hp[:, l * V7X_LANES:(l + 1) * V7X_LANES]
    hs = jnp.concatenate(
        [jnp.concatenate([slabs[l][_segment_rows(gi, seg), :] for l in range(nw)], axis=-1)
         for gi in range(seg)], axis=0)

    y = (_gelu(gate) * hs).astype(BF16)
    o_ref[0] = x + jnp.dot(y, wo_ref[...], preferred_element_type=F32)


def _rglru(x, g, w_in, conv_w, conv_b, w_a, b_a, w_x, b_x, lam, w_out):
    b, s, d = x.shape
    width = w_out.shape[0]
    nblk = w_a.shape[0]
    ts = SEQ_TILE
    wax = jnp.concatenate([w_a, w_x], axis=-1).astype(BF16)
    kern = functools.partial(_rglru_kernel, width=width, nblk=nblk)
    return pl.pallas_call(
        kern,
        out_shape=jax.ShapeDtypeStruct((b, s, d), F32),
        grid=(b, s // ts),
        in_specs=[
            pl.BlockSpec((1, ts, d), lambda i, j: (i, j, 0)),
            _const_spec((1, d)),
            _const_spec((d, 2 * width)),
            _const_spec(conv_w.shape),
            _const_spec((1, width)),
            _const_spec(wax.shape),
            _const_spec((1, width)),
            _const_spec((1, width)),
            _const_spec((1, width)),
            _const_spec((width, d)),
        ],
        out_specs=pl.BlockSpec((1, ts, d), lambda i, j: (i, j, 0)),
        scratch_shapes=[
            pltpu.VMEM(((conv_w.shape[0] - 1) * V7X_SUBLANES, width), F32),
            pltpu.VMEM((V7X_SUBLANES, width), F32),
        ] + [pltpu.VMEM((ts, V7X_LANES), F32)] * (width // V7X_LANES),
        compiler_params=_params(("arbitrary", "arbitrary")),
        name="rglru_mixer",
    )(x, g.reshape(1, d), w_in.astype(BF16), conv_w, conv_b.reshape(1, width), wax,
      b_a.reshape(1, width), b_x.reshape(1, width), lam.reshape(1, width), w_out.astype(BF16))


def _pa_in_kernel(x_ref, g_ref, wu_ref, wk_ref, wqt_ref, wvt_ref, pw_ref, ps_ref, qg_ref, kg_ref, e_ref,
                  p_ref, qt_ref, k_ref, vt_ref, km_ref, ucarry_scr):
    ts = x_ref.shape[1]
    nhp = qt_ref.shape[1]
    nbt = ts // MOBA_BLOCK
    s = pl.program_id(1)
    h = _rms(x_ref[0], g_ref[...]).astype(BF16)

    @pl.when(s == 0)
    def _():
        ucarry_scr[...] = jnp.zeros_like(ucarry_scr)

    u = jnp.dot(h, wu_ref[...], preferred_element_type=F32)
    halo = ucarry_scr.shape[0]
    ext = jnp.concatenate([ucarry_scr[...], u], axis=0)
    ucarry_scr[...] = u[ts - halo:, :]
    pos = s * ts + lax.broadcasted_iota(jnp.int32, (ts, 1), 0)
    for gi, w in enumerate(POOL_WINDOWS):
        lanes = slice(gi * POOL_GROUP, (gi + 1) * POOL_GROUP)
        acc = ext[:, lanes]
        sh = 1
        while sh < w:
            acc = acc + pltpu.roll(acc, sh, 0)
            sh *= 2
        cnt = jnp.minimum(pos + 1, w).astype(F32)
        pg = acc[halo:] / cnt - u[:, lanes]
        out = jnp.dot(pg.astype(BF16), pw_ref[gi], preferred_element_type=F32) * ps_ref[:, lanes]
        p_ref[0, :, lanes] = out.astype(BF16)

    nt = (((1,), (1,)), ((), ()))
    nh = wqt_ref.shape[0] // HEAD_DIM
    qt = lax.dot_general(wqt_ref[...], h, nt, preferred_element_type=F32)
    q3 = qt.reshape(nh, HEAD_DIM, ts)
    ms = jnp.sum(q3 * q3, axis=1, keepdims=True) * (1.0 / HEAD_DIM)
    qn = ((q3 * lax.rsqrt(ms + EPS)).reshape(nh * HEAD_DIM, ts) * qg_ref[...]).astype(BF16)
    vt = lax.dot_general(wvt_ref[...], h, nt, preferred_element_type=F32).astype(BF16)

    kk = jnp.dot(h, wk_ref[...], preferred_element_type=F32)
    ssq = jnp.dot((kk * kk).astype(BF16), e_ref[...], preferred_element_type=F32)
    kn = kk * lax.rsqrt(ssq * (1.0 / HEAD_DIM) + EPS) * kg_ref[...]
    knb = kn.astype(BF16)

    hpw = 2 * HEAD_DIM
    for jb in range(nbt):
        cols = slice(jb * MOBA_BLOCK, (jb + 1) * MOBA_BLOCK)
        kmean = jnp.sum(kn[cols, :], axis=0, keepdims=True) * (1.0 / MOBA_BLOCK)
        for hp in range(nhp):
            rows = slice(hp * hpw, (hp + 1) * hpw)
            qt_ref[0, hp, jb] = qn[rows, cols]
            vt_ref[0, hp, jb] = vt[rows, cols]
            k_ref[0, hp, jb] = knb[cols, rows]
            km_ref[0, hp, pl.ds(s * nbt + jb, 1), :] = kmean[:, rows]


def _pa_in(x, g, w_in, pool_w, pool_scale, q_gain, k_gain):
    b, s, d = x.shape
    pw = pool_w.shape[0] * POOL_GROUP
    aw = (w_in.shape[1] - pw) // 3
    nh = aw // HEAD_DIM
    nhp = nh // 2
    nb = s // MOBA_BLOCK
    ts = SEQ_TILE
    hpw = 2 * HEAD_DIM
    wu = w_in[:, :pw].astype(BF16)
    wqt = w_in[:, pw:pw + aw].T.astype(BF16)
    wk = w_in[:, pw + aw:pw + 2 * aw].astype(BF16)
    wvt = w_in[:, pw + 2 * aw:].T.astype(BF16)
    qg = (jnp.tile(q_gain, nh) * (HEAD_DIM ** -0.5)).reshape(aw, 1)
    kg = jnp.tile(k_gain, nh).reshape(1, aw)
    hid = jnp.arange(aw) // HEAD_DIM
    e = (hid[:, None] == hid[None, :]).astype(BF16)
    return pl.pallas_call(
        _pa_in_kernel,
        out_shape=(
            jax.ShapeDtypeStruct((b, s, pw), BF16),
            jax.ShapeDtypeStruct((b, nhp, nb, hpw, MOBA_BLOCK), BF16),
            jax.ShapeDtypeStruct((b, nhp, nb, MOBA_BLOCK, hpw), BF16),
            jax.ShapeDtypeStruct((b, nhp, nb, hpw, MOBA_BLOCK), BF16),
            jax.ShapeDtypeStruct((b, nhp, nb, hpw), F32),
        ),
        grid=(b, s // ts),
        in_specs=[
            pl.BlockSpec((1, ts, d), lambda i, j: (i, j, 0)),
            _const_spec((1, d)),
            _const_spec((d, pw)),
            _const_spec((d, aw)),
            _const_spec((aw, d)),
            _const_spec((aw, d)),
            _const_spec(pool_w.shape),
            _const_spec((1, pw)),
            _const_spec((aw, 1)),
            _const_spec((1, aw)),
            _const_spec((aw, aw)),
        ],
        out_specs=(
            pl.BlockSpec((1, ts, pw), lambda i, j: (i, j, 0)),
            pl.BlockSpec((1, nhp, ts // MOBA_BLOCK, hpw, MOBA_BLOCK), lambda i, j: (i, 0, j, 0, 0)),
            pl.BlockSpec((1, nhp, ts // MOBA_BLOCK, MOBA_BLOCK, hpw), lambda i, j: (i, 0, j, 0, 0)),
            pl.BlockSpec((1, nhp, ts // MOBA_BLOCK, hpw, MOBA_BLOCK), lambda i, j: (i, 0, j, 0, 0)),
            pl.BlockSpec((1, nhp, nb, hpw), lambda i, j: (i, 0, 0, 0)),
        ),
        scratch_shapes=[pltpu.VMEM((max(POOL_WINDOWS), pw), F32)],
        compiler_params=_params(("arbitrary", "arbitrary")),
        name="pool_attn_in",
    )(x, g.reshape(1, d), wu, wk, wqt, wvt, pool_w.astype(BF16), pool_scale.reshape(1, pw), qg, kg, e)


def _moba_kernel(qt_ref, k_ref, vt_ref, km_ref, bias_ref, jsl_ref, o_ref, pv_scr, m_scr, w_scr):
    nqb = qt_ref.shape[2]
    nb = k_ref.shape[2]
    hpw, bq = qt_ref.shape[3], qt_ref.shape[4]
    iq = [pl.program_id(2) * nqb + qi for qi in range(nqb)]
    hrow = lax.broadcasted_iota(jnp.int32, (hpw, bq), 0) < HEAD_DIM
    mine = (hrow, jnp.logical_not(hrow))
    jj = lax.broadcasted_iota(jnp.int32, (nb, bq), 0)
    kmb = km_ref[0, 0].astype(BF16)
    zeros = jnp.zeros((hpw, bq), BF16)
    qh = [[jnp.where(mine[hh], qt_ref[0, 0, qi], zeros) for hh in range(2)] for qi in range(nqb)]
    m_scr[...] = jnp.zeros_like(m_scr)

    for qi in range(nqb):
        for hh in range(2):
            gate = jnp.dot(kmb, qh[qi][hh], preferred_element_type=F32)
            gt = jnp.where(jj < iq[qi], gate, -jnp.inf)
            rank = jnp.zeros((nb, bq), jnp.int32)
            for m in range(nb):
                gm = gt[m:m + 1, :]
                ahead = (gm > gt) | ((gm == gt) & (jj > m))
                rank = rank + ahead.astype(jnp.int32)
            keep = ((jj < iq[qi]) & (rank < MOBA_TOPK)) | (jj == iq[qi])
            w_scr[qi, hh] = jnp.where(keep, 0.0, NEG_INF) + jsl_ref[hh]

    def block(j, own=None):
        kj = k_ref[0, 0, j]
        vj = vt_ref[0, 0, j]
        for hh in range(2):
            vh = jnp.where(mine[hh], vj, jnp.ones_like(vj))
            for qi in range(0 if own is None else own, nqb):
                kind = 1 if qi == own else 0
                for qs in range(0, bq, QUERY_SPLIT):
                    ql = slice(qs, qs + QUERY_SPLIT)
                    st = jnp.dot(kj, qh[qi][hh][:, ql], preferred_element_type=F32) + bias_ref[kind, hh, :, ql]
                    mx = jnp.max(st, axis=0, keepdims=True)
                    p = jnp.exp(st - mx).astype(BF16)
                    pv_scr[j, qi, hh, :, ql] = jnp.dot(vh, p, preferred_element_type=F32)
                    m_scr[qi, hh, qs // QUERY_SPLIT, pl.ds(j, 1), :] = mx

    def group(t, carry):
        for u in range(nqb):
            block(nqb * t + u)
        return carry

    lax.fori_loop(0, pl.program_id(2), group, 0)
    for u in range(nqb):
        block(iq[u], own=u)

    ext = HEAD_DIM + V7X_SUBLANES
    rows = (slice(0, ext), slice(hpw - ext, hpw))

    def comb(j, acc, qi, hh):
        return acc + pv_scr[j, qi, hh, rows[hh], :] * w_scr[qi, hh, pl.ds(j, 1), :]

    for qi in range(nqb):
        accs = []
        for hh in range(2):
            mr = jnp.concatenate([m_scr[qi, hh, t] for t in range(bq // QUERY_SPLIT)], axis=-1) + w_scr[qi, hh]
            w_scr[qi, hh] = jnp.exp(mr - jnp.max(mr, axis=0, keepdims=True))
            accs.append(lax.fori_loop(0, iq[qi] + 1, functools.partial(comb, qi=qi, hh=hh),
                                      jnp.zeros((ext, bq), F32)))
        ot = jnp.concatenate([accs[0][:HEAD_DIM] / accs[0][HEAD_DIM:HEAD_DIM + 1],
                              accs[1][V7X_SUBLANES:] / accs[1][0:1]], axis=0)
        o_ref[0, qi * bq:(qi + 1) * bq, :] = ot.T.astype(BF16)


def _moba(qt, k, vt, km):
    b, nhp, nb, hpw, bq = qt.shape
    nh = 2 * nhp
    nqb = Q_BLOCKS
    slopes = 2.0 ** (-8.0 * jnp.arange(1, nh + 1, dtype=F32) / nh)
    r = jnp.arange(MOBA_BLOCK, dtype=F32)
    past = slopes[:, None, None] * jnp.broadcast_to(r[:, None], (MOBA_BLOCK, bq))[None]
    causal = jnp.arange(MOBA_BLOCK)[:, None] <= jnp.arange(bq)[None, :]
    bias = jnp.stack([past, past + jnp.where(causal, 0.0, NEG_INF)[None]])
    jsl = slopes[:, None, None] * jnp.broadcast_to(
        (jnp.arange(nb, dtype=F32) * MOBA_BLOCK)[:, None], (nb, bq))[None]
    return pl.pallas_call(
        _moba_kernel,
        out_shape=jax.ShapeDtypeStruct((b, nb * bq, nhp * hpw), BF16),
        grid=(b, nhp, nb // nqb),
        in_specs=[
            pl.BlockSpec((1, 1, nqb, hpw, bq), lambda bi, hp, i: (bi, hp, i, 0, 0)),
            pl.BlockSpec((1, 1, nb, MOBA_BLOCK, hpw), lambda bi, hp, i: (bi, hp, 0, 0, 0)),
            pl.BlockSpec((1, 1, nb, hpw, MOBA_BLOCK), lambda bi, hp, i: (bi, hp, 0, 0, 0)),
            pl.BlockSpec((1, 1, nb, hpw), lambda bi, hp, i: (bi, hp, 0, 0)),
            pl.BlockSpec((2, 2, MOBA_BLOCK, bq), lambda bi, hp, i: (0, hp, 0, 0)),
            pl.BlockSpec((2, nb, bq), lambda bi, hp, i: (hp, 0, 0)),
        ],
        out_specs=pl.BlockSpec((1, nqb * bq, hpw), lambda bi, hp, i: (bi, i, hp)),
        scratch_shapes=[
            pltpu.VMEM((nb, nqb, 2, hpw, bq), F32),
            pltpu.VMEM((nqb, 2, bq // QUERY_SPLIT, nb, QUERY_SPLIT), F32),
            pltpu.VMEM((nqb, 2, nb, bq), F32),
        ],
        compiler_params=_params(("arbitrary", "arbitrary", "arbitrary")),
        name="moba_attention",
    )(qt, k, vt, km, bias, jsl)


@jax.jit
def kernel(x, norm_mix, norm_ffn, pa_w_in, pa_pool_w, pa_pool_scale, pa_q_gain, pa_k_gain, pa_w_out,
           rg_w_in, rg_conv_w, rg_conv_b, rg_w_a, rg_b_a, rg_w_x, rg_b_x, rg_lambda, rg_w_out,
           ffn_w_in, ffn_conv_w, ffn_conv_b, ffn_w_out):
    depth = norm_mix.shape[0]
    for l in range(depth):
        j = l // 2
        residual = ()
        if l % 2 == 0:
            p, qt, k, vt, km = _pa_in(x, norm_mix[l], pa_w_in[j], pa_pool_w[j], pa_pool_scale[j],
                                      pa_q_gain[j], pa_k_gain[j])
            a = _moba(qt, k, vt, km)
            pw = p.shape[-1]
            residual = ((p, pa_w_out[j][:pw]), (a, pa_w_out[j][pw:]))
        else:
            x = _rglru(x, norm_mix[l], rg_w_in[j], rg_conv_w[j], rg_conv_b[j], rg_w_a[j], rg_b_a[j],
                       rg_w_x[j], rg_b_x[j], rg_lambda[j], rg_w_out[j])
        x = _ffn(x, norm_ffn[l], ffn_w_in[l], ffn_conv_w[l], ffn_conv_b[l], ffn_w_out[l], residual)
    return x
```

```python
import functools
import math

import jax
import jax.numpy as jnp
from jax import lax
from jax.experimental import pallas as pl
from jax.experimental.pallas import tpu as pltpu

F32 = jnp.float32
BF16 = jnp.bfloat16

EPS = 1e-6
NEG_INF = -1e30
POOL_WINDOWS = (2, 4, 8, 16)
POOL_GROUP = 128
HEAD_DIM = 64
MOBA_BLOCK = 256
MOBA_TOPK = 3
RG_C = 8.0
RNN_BLOCK = 128

V7X_LANES = 128
V7X_SUBLANES = 8
V7X_VMEM_LIMIT_BYTES = 56 * 1024 * 1024

SEQ_TILE = 512
FFN_TILE = 1024
FFN_CHUNK = 256
QUERY_SPLIT = 128
Q_BLOCKS = 8


def _rms(x, g):
    ms = jnp.mean(x * x, axis=-1, keepdims=True)
    return x * lax.rsqrt(ms + EPS) * g


_GELU_K1 = -2.0 * math.sqrt(2.0 / math.pi) * math.log2(math.e)
_GELU_K2 = _GELU_K1 * 0.044715


def _gelu(x):
    return x / (1.0 + jnp.exp2((x * x * _GELU_K2 + _GELU_K1) * x))


def _const_spec(shape):
    zeros = (0,) * len(shape)
    return pl.BlockSpec(shape, lambda *_: zeros, pipeline_mode=pl.Buffered(1))


def _params(sem):
    return pltpu.CompilerParams(dimension_semantics=sem, vmem_limit_bytes=V7X_VMEM_LIMIT_BYTES)


def _segment_rows(gi, seg):
    s, p = divmod(V7X_SUBLANES * gi, seg)
    return pl.ds(p * V7X_SUBLANES + s, V7X_SUBLANES, stride=V7X_SUBLANES)


def _conv_segment_major(zp, prev_tail, w, bias):
    kw, hl, ts = w.shape[0], V7X_SUBLANES, zp.shape[0]
    sub = lax.broadcasted_iota(jnp.int32, (hl, 1), 0)
    tail = zp[ts - (kw - 1) * hl:]
    halos = []
    for i in range(kw - 1):
        grp = slice(i * hl, (i + 1) * hl)
        halos.append(pltpu.roll(jnp.where(sub == hl - 1, prev_tail[grp], tail[grp]), 1, 0))
    ext = jnp.concatenate(halos + [zp], axis=0)
    y = bias
    for k in range(kw):
        y = y + ext[k * hl:k * hl + ts] * w[k:k + 1]
    return y, tail


def _ffn_kernel(*refs, nch, cw, nres):
    x_ref, refs = refs[0], refs[1:]
    acts, wts, refs = refs[:nres], refs[nres:2 * nres], refs[2 * nres:]
    g_ref, wug_ref, cw_ref, cb_ref, wo_ref, o_ref, h_scr, carry_scr, z0_scr, z1_scr = refs
    z_scr = (z0_scr, z1_scr)
    ts = x_ref.shape[1]
    kw = cw_ref.shape[1]
    hl = V7X_SUBLANES
    x = x_ref[0]
    for a_ref, w_ref in zip(acts, wts):
        x = x + jnp.dot(a_ref[0], w_ref[...], preferred_element_type=F32)
    h_scr[...] = _rms(x, g_ref[...]).astype(BF16)
    o_ref[0] = x

    @pl.when(pl.program_id(1) == 0)
    def _():
        carry_scr[...] = jnp.zeros_like(carry_scr)

    def up(c, slot):
        z_scr[slot][hl:, :] = jnp.dot(h_scr[...], wug_ref[c], preferred_element_type=F32)

    def down(c, slot):
        zb = z_scr[slot]
        zb[0:hl, :] = carry_scr[c]
        carry_scr[c] = zb[ts:ts + hl, :]
        w = cw_ref[c]
        y = cb_ref[c] + zb[hl:, :] * w[kw - 1:kw]
        for k in range(kw - 1):
            off = hl - (kw - 1) + k
            y = y + zb[off:off + ts, :] * w[k:k + 1]
        a = (_gelu(y[:, cw:]) * y[:, :cw]).astype(BF16)
        o_ref[0] += jnp.dot(a, wo_ref[c], preferred_element_type=F32)

    up(0, 0)

    def pair(q, carry):
        c = 2 * q
        up(c + 1, 1)
        down(c, 0)
        up(c + 2, 0)
        down(c + 1, 1)
        return carry

    npairs = (nch - 1) // 2
    lax.fori_loop(0, npairs, pair, 0)
    if nch % 2 == 0:
        up(nch - 1, 1)
        down(nch - 2, 0)
        down(nch - 1, 1)
    else:
        down(nch - 1, 0)


def _ffn(x, g, w_in, conv_w, conv_b, w_out, residual=()):
    b, s, d = x.shape
    dff = w_out.shape[0]
    cw = FFN_CHUNK
    nch = dff // cw
    ts = FFN_TILE
    wu = w_in[:, :dff].reshape(d, nch, cw)
    wg = w_in[:, dff:].reshape(d, nch, cw)
    wug = jnp.concatenate([wu, wg], axis=-1).transpose(1, 0, 2).astype(BF16)
    cwu = conv_w[:, :dff].reshape(-1, nch, cw)
    cwg = conv_w[:, dff:].reshape(-1, nch, cw)
    cwr = jnp.concatenate([cwu, cwg], axis=-1).transpose(1, 0, 2)
    cbr = jnp.concatenate([conv_b[:dff].reshape(nch, 1, cw), conv_b[dff:].reshape(nch, 1, cw)], axis=-1)
    wo = w_out.reshape(nch, cw, d).astype(BF16)
    kern = functools.partial(_ffn_kernel, nch=nch, cw=cw, nres=len(residual))
    acts = [a for a, _ in residual]
    wts = [w.astype(BF16) for _, w in residual]
    return pl.pallas_call(
        kern,
        out_shape=jax.ShapeDtypeStruct((b, s, d), F32),
        grid=(b, s // ts),
        in_specs=[pl.BlockSpec((1, ts, d), lambda i, j: (i, j, 0))]
        + [pl.BlockSpec((1, ts, a.shape[-1]), lambda i, j: (i, j, 0)) for a in acts]
        + [_const_spec(w.shape) for w in wts]
        + [
            _const_spec((1, d)),
            _const_spec((nch, d, 2 * cw)),
            _const_spec((nch, conv_w.shape[0], 2 * cw)),
            _const_spec((nch, 1, 2 * cw)),
            _const_spec((nch, cw, d)),
        ],
        out_specs=pl.BlockSpec((1, ts, d), lambda i, j: (i, j, 0)),
        scratch_shapes=[
            pltpu.VMEM((ts, d), BF16),
            pltpu.VMEM((nch, V7X_SUBLANES, 2 * cw), F32),
            pltpu.VMEM((V7X_SUBLANES + ts, 2 * cw), F32),
            pltpu.VMEM((V7X_SUBLANES + ts, 2 * cw), F32),
        ],
        compiler_params=_params(("arbitrary", "arbitrary")),
        name="conv_ffn",
    )(x, *acts, *wts, g.reshape(1, d), wug, cwr, cbr, wo)


def _rglru_kernel(x_ref, g_ref, win_ref, cw_ref, cb_ref, wax_ref, ba_ref, bx_ref, lam_ref, wo_ref,
                  o_ref, xcarry_scr, hcarry_scr, *slabs, width, nblk):
    ts = x_ref.shape[1]
    hl = V7X_SUBLANES
    seg = ts // hl
    nw = width // V7X_LANES
    x = x_ref[0]
    h = _rms(x, g_ref[...]).astype(BF16)

    @pl.when(pl.program_id(1) == 0)
    def _():
        xcarry_scr[...] = jnp.zeros_like(xcarry_scr)
        hcarry_scr[...] = jnp.zeros_like(hcarry_scr)

    z = jnp.dot(h, win_ref[...], preferred_element_type=F32)
    gate = z[:, :width]
    for gi in range(seg):
        t0 = hl * gi
        for l in range(nw):
            lanes = slice(width + l * V7X_LANES, width + (l + 1) * V7X_LANES)
            slabs[l][_segment_rows(gi, seg), :] = z[t0:t0 + hl, lanes]
    xr = jnp.concatenate([slabs[l][...] for l in range(nw)], axis=-1)
    xc, tail = _conv_segment_major(xr, xcarry_scr[...], cw_ref[...], cb_ref[...])
    xcarry_scr[...] = tail

    xcb = xc.astype(BF16)
    ra, rx = [], []
    for n in range(nblk):
        ax = jnp.dot(xcb[:, n * RNN_BLOCK:(n + 1) * RNN_BLOCK], wax_ref[n], preferred_element_type=F32)
        ra.append(ax[:, :RNN_BLOCK])
        rx.append(ax[:, RNN_BLOCK:])
    r = jax.nn.sigmoid(jnp.concatenate(ra, axis=-1) + ba_ref[...])
    ig = jax.nn.sigmoid(jnp.concatenate(rx, axis=-1) + bx_ref[...])
    log_a = (-RG_C * jax.nn.softplus(-lam_ref[...])) * r
    a = jnp.exp(log_a)
    om = 1.0 - a * a
    bb = jnp.where(om > 0.0, om * lax.rsqrt(om), 0.0) * (ig * xc)

    a3 = a.reshape(seg, hl, width)
    b3 = bb.reshape(seg, hl, width)
    hloc, ploc = [b3[0]], [a3[0]]
    for p in range(1, seg):
        hloc.append(a3[p] * hloc[-1] + b3[p])
        ploc.append(a3[p] * ploc[-1])
    sub = lax.broadcasted_iota(jnp.int32, (hl, 1), 0)
    base = pltpu.roll(hcarry_scr[...], 1, 0)
    enter = base
    for _ in range(hl - 1):
        enter = jnp.where(sub == 0, base, pltpu.roll(hloc[-1] + ploc[-1] * enter, 1, 0))
    hcarry_scr[...] = hloc[-1] + ploc[-1] * enter
    for p in range(seg):
        hp = hloc[p] + ploc[p] * enter
        for l in range(nw):
            slabs[l][p * hl:(p + 1) * hl, :] = hp[:, l * V7X_LANES:(l + 1) * V7X_LANES]
    hs = jnp.concatenate(
        [jnp.concatenate([slabs[l][_segment_rows(gi, seg), :] for l in range(nw)], axis=-1)
         for gi in range(seg)], axis=0)

    y = (_gelu(gate) * hs).astype(BF16)
    o_ref[0] = x + jnp.dot(y, wo_ref[...], preferred_element_type=F32)


def _rglru(x, g, w_in, conv_w, conv_b, w_a, b_a, w_x, b_x, lam, w_out):
    b, s, d = x.shape
    width = w_out.shape[0]
    nblk = w_a.shape[0]
    ts = SEQ_TILE
    wax = jnp.concatenate([w_a, w_x], axis=-1).astype(BF16)
    kern = functools.partial(_rglru_kernel, width=width, nblk=nblk)
    return pl.pallas_call(
        kern,
        out_shape=jax.ShapeDtypeStruct((b, s, d), F32),
        grid=(b, s // ts),
        in_specs=[
            pl.BlockSpec((1, ts, d), lambda i, j: (i, j, 0)),
            _const_spec((1, d)),
            _const_spec((d, 2 * width)),
            _const_spec(conv_w.shape),
            _const_spec((1, width)),
            _const_spec(wax.shape),
            _const_spec((1, width)),
            _const_spec((1, width)),
            _const_spec((1, width)),
            _const_spec((width, d)),
        ],
        out_specs=pl.BlockSpec((1, ts, d), lambda i, j: (i, j, 0)),
        scratch_shapes=[
            pltpu.VMEM(((conv_w.shape[0] - 1) * V7X_SUBLANES, width), F32),
            pltpu.VMEM((V7X_SUBLANES, width), F32),
        ] + [pltpu.VMEM((ts, V7X_LANES), F32)] * (width // V7X_LANES),
        compiler_params=_params(("arbitrary", "arbitrary")),
        name="rglru_mixer",
    )(x, g.reshape(1, d), w_in.astype(BF16), conv_w, conv_b.reshape(1, width), wax,
      b_a.reshape(1, width), b_x.reshape(1, width), lam.reshape(1, width), w_out.astype(BF16))


def _pa_in_kernel(x_ref, g_ref, wu_ref, wk_ref, wqt_ref, wvt_ref, pw_ref, ps_ref, qg_ref, kg_ref, e_ref,
                  p_ref, qt_ref, k_ref, vt_ref, km_ref, ucarry_scr):
    ts = x_ref.shape[1]
    nhp = qt_ref.shape[1]
    nbt = ts // MOBA_BLOCK
    s = pl.program_id(1)
    h = _rms(x_ref[0], g_ref[...]).astype(BF16)

    @pl.when(s == 0)
    def _():
        ucarry_scr[...] = jnp.zeros_like(ucarry_scr)

    u = jnp.dot(h, wu_ref[...], preferred_element_type=F32)
    halo = ucarry_scr.shape[0]
    ext = jnp.concatenate([ucarry_scr[...], u], axis=0)
    ucarry_scr[...] = u[ts - halo:, :]
    pos = s * ts + lax.broadcasted_iota(jnp.int32, (ts, 1), 0)
    for gi, w in enumerate(POOL_WINDOWS):
        lanes = slice(gi * POOL_GROUP, (gi + 1) * POOL_GROUP)
        acc = ext[:, lanes]
        sh = 1
        while sh < w:
            acc = acc + pltpu.roll(acc, sh, 0)
            sh *= 2
        cnt = jnp.minimum(pos + 1, w).astype(F32)
        pg = acc[halo:] / cnt - u[:, lanes]
        out = jnp.dot(pg.astype(BF16), pw_ref[gi], preferred_element_type=F32) * ps_ref[:, lanes]
        p_ref[0, :, lanes] = out.astype(BF16)

    nt = (((1,), (1,)), ((), ()))
    nh = wqt_ref.shape[0] // HEAD_DIM
    qt = lax.dot_general(wqt_ref[...], h, nt, preferred_element_type=F32)
    q3 = qt.reshape(nh, HEAD_DIM, ts)
    ms = jnp.sum(q3 * q3, axis=1, keepdims=True) * (1.0 / HEAD_DIM)
    qn = ((q3 * lax.rsqrt(ms + EPS)).reshape(nh * HEAD_DIM, ts) * qg_ref[...]).astype(BF16)
    vt = lax.dot_general(wvt_ref[...], h, nt, preferred_element_type=F32).astype(BF16)

    kk = jnp.dot(h, wk_ref[...], preferred_element_type=F32)
    ssq = jnp.dot((kk * kk).astype(BF16), e_ref[...], preferred_element_type=F32)
    kn = kk * lax.rsqrt(ssq * (1.0 / HEAD_DIM) + EPS) * kg_ref[...]
    knb = kn.astype(BF16)

    hpw = 2 * HEAD_DIM
    for jb in range(nbt):
        cols = slice(jb * MOBA_BLOCK, (jb + 1) * MOBA_BLOCK)
        kmean = jnp.sum(kn[cols, :], axis=0, keepdims=True) * (1.0 / MOBA_BLOCK)
        for hp in range(nhp):
            rows = slice(hp * hpw, (hp + 1) * hpw)
            qt_ref[0, hp, jb] = qn[rows, cols]
            vt_ref[0, hp, jb] = vt[rows, cols]
            k_ref[0, hp, jb] = knb[cols, rows]
            km_ref[0, hp, pl.ds(s * nbt + jb, 1), :] = kmean[:, rows]


def _pa_in(x, g, w_in, pool_w, pool_scale, q_gain, k_gain):
    b, s, d = x.shape
    pw = pool_w.shape[0] * POOL_GROUP
    aw = (w_in.shape[1] - pw) // 3
    nh = aw // HEAD_DIM
    nhp = nh // 2
    nb = s // MOBA_BLOCK
    ts = SEQ_TILE
    hpw = 2 * HEAD_DIM
    wu = w_in[:, :pw].astype(BF16)
    wqt = w_in[:, pw:pw + aw].T.astype(BF16)
    wk = w_in[:, pw + aw:pw + 2 * aw].astype(BF16)
    wvt = w_in[:, pw + 2 * aw:].T.astype(BF16)
    qg = (jnp.tile(q_gain, nh) * (HEAD_DIM ** -0.5)).reshape(aw, 1)
    kg = jnp.tile(k_gain, nh).reshape(1, aw)
    hid = jnp.arange(aw) // HEAD_DIM
    e = (hid[:, None] == hid[None, :]).astype(BF16)
    return pl.pallas_call(
        _pa_in_kernel,
        out_shape=(
            jax.ShapeDtypeStruct((b, s, pw), BF16),
            jax.ShapeDtypeStruct((b, nhp, nb, hpw, MOBA_BLOCK), BF16),
            jax.ShapeDtypeStruct((b, nhp, nb, MOBA_BLOCK, hpw), BF16),
            jax.ShapeDtypeStruct((b, nhp, nb, hpw, MOBA_BLOCK), BF16),
            jax.ShapeDtypeStruct((b, nhp, nb, hpw), F32),
        ),
        grid=(b, s // ts),
        in_specs=[
            pl.BlockSpec((1, ts, d), lambda i, j: (i, j, 0)),
            _const_spec((1, d)),
            _const_spec((d, pw)),
            _const_spec((d, aw)),
            _const_spec((aw, d)),
            _const_spec((aw, d)),
            _const_spec(pool_w.shape),
            _const_spec((1, pw)),
            _const_spec((aw, 1)),
            _const_spec((1, aw)),
            _const_spec((aw, aw)),
        ],
        out_specs=(
            pl.BlockSpec((1, ts, pw), lambda i, j: (i, j, 0)),
            pl.BlockSpec((1, nhp, ts // MOBA_BLOCK, hpw, MOBA_BLOCK), lambda i, j: (i, 0, j, 0, 0)),
            pl.BlockSpec((1, nhp, ts // MOBA_BLOCK, MOBA_BLOCK, hpw), lambda i, j: (i, 0, j, 0, 0)),
            pl.BlockSpec((1, nhp, ts // MOBA_BLOCK, hpw, MOBA_BLOCK), lambda i, j: (i, 0, j, 0, 0)),
            pl.BlockSpec((1, nhp, nb, hpw), lambda i, j: (i, 0, 0, 0)),
        ),
        scratch_shapes=[pltpu.VMEM((max(POOL_WINDOWS), pw), F32)],
        compiler_params=_params(("arbitrary", "arbitrary")),
        name="pool_attn_in",
    )(x, g.reshape(1, d), wu, wk, wqt, wvt, pool_w.astype(BF16), pool_scale.reshape(1, pw), qg, kg, e)


def _moba_kernel(qt_ref, k_ref, vt_ref, km_ref, bias_ref, jsl_ref, o_ref, pv_scr, m_scr, w_scr):
    nqb = qt_ref.shape[2]
    nb = k_ref.shape[2]
    hpw, bq = qt_ref.shape[3], qt_ref.shape[4]
    iq = [pl.program_id(2) * nqb + qi for qi in range(nqb)]
    hrow = lax.broadcasted_iota(jnp.int32, (hpw, bq), 0) < HEAD_DIM
    mine = (hrow, jnp.logical_not(hrow))
    jj = lax.broadcasted_iota(jnp.int32, (nb, bq), 0)
    kmb = km_ref[0, 0].astype(BF16)
    zeros = jnp.zeros((hpw, bq), BF16)
    qh = [[jnp.where(mine[hh], qt_ref[0, 0, qi], zeros) for hh in range(2)] for qi in range(nqb)]
    m_scr[...] = jnp.zeros_like(m_scr)

    for qi in range(nqb):
        for hh in range(2):
            gate = jnp.dot(kmb, qh[qi][hh], preferred_element_type=F32)
            gt = jnp.where(jj < iq[qi], gate, -jnp.inf)
            rank = jnp.zeros((nb, bq), jnp.int32)
            for m in range(nb):
                gm = gt[m:m + 1, :]
                ahead = (gm > gt) | ((gm == gt) & (jj > m))
                rank = rank + ahead.astype(jnp.int32)
            keep = ((jj < iq[qi]) & (rank < MOBA_TOPK)) | (jj == iq[qi])
            w_scr[qi, hh] = jnp.where(keep, 0.0, NEG_INF) + jsl_ref[hh]

    def block(j, own=None):
        kj = k_ref[0, 0, j]
        vj = vt_ref[0, 0, j]
        for hh in range(2):
            vh = jnp.where(mine[hh], vj, jnp.ones_like(vj))
            for qi in range(0 if own is None else own, nqb):
                kind = 1 if qi == own else 0
                for qs in range(0, bq, QUERY_SPLIT):
                    ql = slice(qs, qs + QUERY_SPLIT)
                    st = jnp.dot(kj, qh[qi][hh][:, ql], preferred_element_type=F32) + bias_ref[kind, hh, :, ql]
                    mx = jnp.max(st, axis=0, keepdims=True)
                    p = jnp.exp(st - mx).astype(BF16)
                    pv_scr[j, qi, hh, :, ql] = jnp.dot(vh, p, preferred_element_type=F32)
                    m_scr[qi, hh, qs // QUERY_SPLIT, pl.ds(j, 1), :] = mx

    def group(t, carry):
        for u in range(nqb):
            block(nqb * t + u)
        return carry

    lax.fori_loop(0, pl.program_id(2), group, 0)
    for u in range(nqb):
        block(iq[u], own=u)

    ext = HEAD_DIM + V7X_SUBLANES
    rows = (slice(0, ext), slice(hpw - ext, hpw))

    def comb(j, acc, qi, hh):
        return acc + pv_scr[j, qi, hh, rows[hh], :] * w_scr[qi, hh, pl.ds(j, 1), :]

    for qi in range(nqb):
        accs = []
        for hh in range(2):
            mr = jnp.concatenate([m_scr[qi, hh, t] for t in range(bq // QUERY_SPLIT)], axis=-1) + w_scr[qi, hh]
            w_scr[qi, hh] = jnp.exp(mr - jnp.max(mr, axis=0, keepdims=True))
            accs.append(lax.fori_loop(0, iq[qi] + 1, functools.partial(comb, qi=qi, hh=hh),
                                      jnp.zeros((ext, bq), F32)))
        ot = jnp.concatenate([accs[0][:HEAD_DIM] / accs[0][HEAD_DIM:HEAD_DIM + 1],
                              accs[1][V7X_SUBLANES:] / accs[1][0:1]], axis=0)
        o_ref[0, qi * bq:(qi + 1) * bq, :] = ot.T.astype(BF16)


def _moba(qt, k, vt, km):
    b, nhp, nb, hpw, bq = qt.shape
    nh = 2 * nhp
    nqb = Q_BLOCKS
    slopes = 2.0 ** (-8.0 * jnp.arange(1, nh + 1, dtype=F32) / nh)
    r = jnp.arange(MOBA_BLOCK, dtype=F32)
    past = slopes[:, None, None] * jnp.broadcast_to(r[:, None], (MOBA_BLOCK, bq))[None]
    causal = jnp.arange(MOBA_BLOCK)[:, None] <= jnp.arange(bq)[None, :]
    bias = jnp.stack([past, past + jnp.where(causal, 0.0, NEG_INF)[None]])
    jsl = slopes[:, None, None] * jnp.broadcast_to(
        (jnp.arange(nb, dtype=F32) * MOBA_BLOCK)[:, None], (nb, bq))[None]
    return pl.pallas_call(
        _moba_kernel,
        out_shape=jax.ShapeDtypeStruct((b, nb * bq, nhp * hpw), BF16),
        grid=(b, nhp, nb // nqb),
        in_specs=[
            pl.BlockSpec((1, 1, nqb, hpw, bq), lambda bi, hp, i: (bi, hp, i, 0, 0)),
            pl.BlockSpec((1, 1, nb, MOBA_BLOCK, hpw), lambda bi, hp, i: (bi, hp, 0, 0, 0)),
            pl.BlockSpec((1, 1, nb, hpw, MOBA_BLOCK), lambda bi, hp, i: (bi, hp, 0, 0, 0)),
            pl.BlockSpec((1, 1, nb, hpw), lambda bi, hp, i: (bi, hp, 0, 0)),
            pl.BlockSpec((2, 2, MOBA_BLOCK, bq), lambda bi, hp, i: (0, hp, 0, 0)),
            pl.BlockSpec((2, nb, bq), lambda bi, hp, i: (hp, 0, 0)),
        ],
        out_specs=pl.BlockSpec((1, nqb * bq, hpw), lambda bi, hp, i: (bi, i, hp)),
        scratch_shapes=[
            pltpu.VMEM((nb, nqb, 2, hpw, bq), F32),
            pltpu.VMEM((nqb, 2, bq // QUERY_SPLIT, nb, QUERY_SPLIT), F32),
            pltpu.VMEM((nqb, 2, nb, bq), F32),
        ],
        compiler_params=_params(("arbitrary", "arbitrary", "arbitrary")),
        name="moba_attention",
    )(qt, k, vt, km, bias, jsl)


@jax.jit
def kernel(x, norm_mix, norm_ffn, pa_w_in, pa_pool_w, pa_pool_scale, pa_q_gain, pa_k_gain, pa_w_out,
           rg_w_in, rg_conv_w, rg_conv_b, rg_w_a, rg_b_a, rg_w_x, rg_b_x, rg_lambda, rg_w_out,
           ffn_w_in, ffn_conv_w, ffn_conv_b, ffn_w_out):
    depth = norm_mix.shape[0]
    for l in range(depth):
        j = l // 2
        residual = ()
        if l % 2 == 0:
            p, qt, k, vt, km = _pa_in(x, norm_mix[l], pa_w_in[j], pa_pool_w[j], pa_pool_scale[j],
                                      pa_q_gain[j], pa_k_gain[j])
            a = _moba(qt, k, vt, km)
            pw = p.shape[-1]
            residual = ((p, pa_w_out[j][:pw]), (a, pa_w_out[j][pw:]))
        else:
            x = _rglru(x, norm_mix[l], rg_w_in[j], rg_conv_w[j], rg_conv_b[j], rg_w_a[j], rg_b_a[j],
                       rg_w_x[j], rg_b_x[j], rg_lambda[j], rg_w_out[j])
        x = _ffn(x, norm_ffn[l], ffn_w_in[l], ffn_conv_w[l], ffn_conv_b[l], ffn_w_out[l], residual)
    return x
```
